```python
import math
import jax, jax.numpy as jnp
from jax import lax
import numpy as np

D_MODEL = 4096
BATCH = 4
SEQ = 2048
DEPTH = 4
DEC_BATCH = 8
DEC_SEQ = 32
PAST_LEN = 1024

CHUNK = 64
N_META = 16
HEAD_DIM = 128
MIX_WIDTH = D_MODEL // 2
H_A = MIX_WIDTH // HEAD_DIM
DK_A = HEAD_DIM
DV_A = HEAD_DIM
CONV_W = 4
GDN_KEY = H_A * DK_A
GDN_VAL = H_A * DV_A
GDN_CONV_DIM = 2 * GDN_KEY + GDN_VAL
H_B = MIX_WIDTH // HEAD_DIM
DH_B = HEAD_DIM
SB_WIDTH = H_B * DH_B
SB_BLOCK = 128
H_C = MIX_WIDTH // HEAD_DIM
DK_C = HEAD_DIM
DV_C = HEAD_DIM
HG_KEY = H_C * DK_C
HG_VAL = H_C * DV_C
N_BRANCH = 3
D_FF = -(-8 * D_MODEL // (3 * 256)) * 256
RMS_EPS = 1e-6
IN_SPLITS = (GDN_CONV_DIM, GDN_VAL, H_A, H_A, 3 * SB_WIDTH, HG_KEY, HG_KEY, HG_VAL, HG_VAL, N_BRANCH * D_MODEL)
N_IN = sum(IN_SPLITS)

kernel_name = "hybrid_gdn_stickbreak_hgrn2_stream_step"

F32 = jnp.float32


def rms_norm(x, w):
    xf = x.astype(F32)
    y = xf * lax.rsqrt(jnp.mean(xf * xf, axis=-1, keepdims=True) + RMS_EPS) * w.astype(F32)
    return y.astype(x.dtype)


def l2_normalize(x):
    return x * lax.rsqrt(jnp.sum(x * x, axis=-1, keepdims=True) + RMS_EPS)


def split_columns(p):
    idx, acc = [], 0
    for w in IN_SPLITS[:-1]:
        acc += w
        idx.append(acc)
    return jnp.split(p, idx, axis=-1)


def hgrn_lower_bounds(p):
    s = jax.nn.softmax(p.astype(F32), axis=0)
    return jnp.cumsum(s, axis=0) - s[0]


def causal_depthwise_conv(x_ext, w):
    c = x_ext.shape[-1]
    return lax.conv_general_dilated(x_ext, w[:, None, :].astype(x_ext.dtype), window_strides=(1,), padding='VALID',
                                    dimension_numbers=('NWC', 'WIO', 'NWC'), feature_group_count=c)


def to_chunks(a, chunk):
    b, t = a.shape[:2]
    n = -(-t // chunk)
    a = jnp.pad(a, [(0, 0), (0, n * chunk - t)] + [(0, 0)] * (a.ndim - 2))
    a = a.reshape((b, n, chunk) + a.shape[2:])
    return a.transpose((1, 0, 3, 2) + tuple(range(4, a.ndim)))


def from_chunks(o, t):
    n, b, h, c, d = o.shape
    return o.transpose(1, 0, 3, 2, 4).reshape(b, n * c, h, d)[:, :t]


def gated_delta_rule(q, k, v, g, beta, s0, chunk):
    t_len = q.shape[1]
    dv = v.shape[-1]
    xs = tuple(to_chunks(a, chunk) for a in (q, k, v, g, beta))
    tri = jnp.tril(jnp.ones((chunk, chunk), bool))
    strict = jnp.tril(jnp.ones((chunk, chunk), bool), -1)
    eye = jnp.eye(chunk, dtype=F32)

    def step(s, inp):
        qi, ki, vi, gi, bi = inp
        gc = jnp.cumsum(gi, axis=-1)
        diff = gc[..., :, None] - gc[..., None, :]
        decay = jnp.where(tri, jnp.exp(jnp.where(tri, diff, 0.0)), 0.0)
        kb = ki * bi[..., None]
        lower = jnp.where(strict, jnp.einsum('bhik,bhjk->bhij', kb, ki) * decay, 0.0)
        rhs = jnp.concatenate([vi * bi[..., None], kb * jnp.exp(gc)[..., None]], axis=-1)
        sol = lax.linalg.triangular_solve(eye + lower, rhs, left_side=True, lower=True, unit_diagonal=True)
        u, w = sol[..., :dv], sol[..., dv:]
        v_new = u - jnp.einsum('bhck,bhkv->bhcv', w, s)
        attn = jnp.einsum('bhik,bhjk->bhij', qi, ki) * decay
        o = jnp.einsum('bhck,bhkv->bhcv', qi * jnp.exp(gc)[..., None], s) + jnp.einsum('bhij,bhjv->bhiv', attn, v_new)
        g_last = gc[..., -1:]
        s = s * jnp.exp(g_last)[..., None] + jnp.einsum('bhck,bhcv->bhkv', ki * jnp.exp(g_last - gc)[..., None], v_new)
        return s, o

    s, o = lax.scan(step, s0, xs)
    return from_chunks(o, t_len), s


def gated_linear_recurrence(q, k, v, log_f, s0, chunk):
    t_len = q.shape[1]
    xs = tuple(to_chunks(a, chunk) for a in (q, k, v, log_f))
    tri = jnp.tril(jnp.ones((chunk, chunk), bool))[..., None]

    def step(s, inp):
        qi, ki, vi, lfi = inp
        b = jnp.cumsum(lfi, axis=2)
        diff = b[:, :, :, None, :] - b[:, :, None, :, :]
        decay = jnp.where(tri, jnp.exp(jnp.where(tri, diff, 0.0)), 0.0)
        attn = jnp.einsum('bhtc,bhsc,bhtsc->bhts', qi, ki, decay)
        o = jnp.einsum('bhts,bhsv->bhtv', attn, vi) + jnp.einsum('bhtc,bhcv->bhtv', qi * jnp.exp(b), s)
        b_last = b[:, :, -1:, :]
        s = jnp.exp(b_last[:, :, 0, :, None]) * s + jnp.einsum('bhsc,bhsv->bhcv', ki * jnp.exp(b_last - b), vi)
        return s, o

    s, o = lax.scan(step, s0, xs)
    return from_chunks(o, t_len), s


def stick_breaking_block(qb, k, v, q_pos, k_pos):
    z = jnp.einsum('bqhd,bkhd->bhqk', qb, k) * (DH_B ** -0.5)
    visible = k_pos[None, :] < q_pos[:, None]
    log_keep = jnp.where(visible, jax.nn.log_sigmoid(-z), 0.0)
    between = lax.cumsum(log_keep, axis=3, reverse=True) - log_keep
    a = jnp.where(visible, jnp.exp(jax.nn.log_sigmoid(z) + between), 0.0)
    return jnp.einsum('bhqk,bkhd->bqhd', a, v)


def stick_breaking_attention(q, k, v, past_len):
    bsz, tq, h, d = q.shape
    blk = min(SB_BLOCK, tq)
    n = -(-tq // blk)
    q_pad = jnp.pad(q, ((0, 0), (0, n * blk - tq), (0, 0), (0, 0)))
    k_pos = jnp.arange(k.shape[1])

    def one_block(i):
        qb = lax.dynamic_slice_in_dim(q_pad, i * blk, blk, axis=1)
        q_pos = past_len + i * blk + jnp.arange(blk)
        return stick_breaking_block(qb, k, v, q_pos, k_pos)

    out = lax.map(one_block, jnp.arange(n))
    return out.transpose(1, 0, 2, 3, 4).reshape(bsz, n * blk, h, d)[:, :tq]


def gated_deltanet_mixer(qkv_raw, z_raw, a_raw, b_raw, conv_hist, state, conv_w, a_log, dt_bias, norm_w, chunk):
    bsz, t_len, _ = qkv_raw.shape
    conv_in = jnp.concatenate([conv_hist.astype(qkv_raw.dtype), qkv_raw], axis=1)
    qkv = jax.nn.silu(causal_depthwise_conv(conv_in, conv_w).astype(F32))
    q, k, v = jnp.split(qkv, [GDN_KEY, 2 * GDN_KEY], axis=-1)
    q = l2_normalize(q.reshape(bsz, t_len, H_A, DK_A)) * (DK_A ** -0.5)
    k = l2_normalize(k.reshape(bsz, t_len, H_A, DK_A))
    v = v.reshape(bsz, t_len, H_A, DV_A)
    beta = jax.nn.sigmoid(b_raw.astype(F32))
    g = -jnp.exp(a_log.astype(F32)) * jax.nn.softplus(a_raw.astype(F32) + dt_bias.astype(F32))
    o, s_new = gated_delta_rule(q, k, v, g, beta, state.astype(F32), chunk)
    o = rms_norm(o, norm_w) * jax.nn.silu(z_raw.astype(F32).reshape(bsz, t_len, H_A, DV_A))
    return o.reshape(bsz, t_len, GDN_VAL), conv_in[:, -(CONV_W - 1):], s_new


def hgrn2_mixer(q_raw, f_raw, i_raw, g_raw, state, lb, norm_w, chunk):
    bsz, t_len, _ = q_raw.shape
    lb = lb.astype(F32)
    q = jax.nn.silu(q_raw.astype(F32)).reshape(bsz, t_len, H_C, DK_C)
    f = lb + (1.0 - lb) * jax.nn.sigmoid(f_raw.astype(F32))
    log_f = jnp.log(f).reshape(bsz, t_len, H_C, DK_C)
    k = (1.0 - f).reshape(bsz, t_len, H_C, DK_C)
    v = i_raw.astype(F32).reshape(bsz, t_len, H_C, DV_C)
    o, s_new = gated_linear_recurrence(q, k, v, log_f, state.astype(F32), chunk)
    o = rms_norm(o, norm_w) * jax.nn.silu(g_raw.astype(F32).reshape(bsz, t_len, H_C, DV_C))
    return o.reshape(bsz, t_len, HG_VAL), s_new


def layer_step(x, past_k, past_v, conv_hist, s_gdn, s_hgrn,
               lb, n1_w, w_in_l, conv_w, a_log, dt_bias, gdn_nw, hg_nw,
               w_br, w_out, n2_w, w_gate, w_up, w_down):
    bsz, t_len, _ = x.shape
    past_len = past_k.shape[1]
    chunk = min(CHUNK, t_len)
    h = rms_norm(x, n1_w)
    proj = h @ w_in_l
    (gdn_qkv, gdn_z, gdn_a, gdn_b, sb_qkv, hg_q, hg_f, hg_i, hg_g, gate_logits) = split_columns(proj)

    o_a, conv_new, s_gdn_new = gated_deltanet_mixer(gdn_qkv, gdn_z, gdn_a, gdn_b, conv_hist, s_gdn,
                                                    conv_w, a_log, dt_bias, gdn_nw, chunk)

    q_b, k_b, v_b = (t.reshape(bsz, t_len, H_B, DH_B) for t in jnp.split(sb_qkv, 3, axis=-1))
    k_all = jnp.concatenate([past_k.astype(F32), k_b.astype(F32)], axis=1)
    v_all = jnp.concatenate([past_v.astype(F32), v_b.astype(F32)], axis=1)
    o_b = stick_breaking_attention(q_b.astype(F32), k_all, v_all, past_len).reshape(bsz, t_len, SB_WIDTH)

    o_c, s_hgrn_new = hgrn2_mixer(hg_q, hg_f, hg_i, hg_g, s_hgrn, lb, hg_nw, chunk)

    branches = jnp.stack([o_a, o_b, o_c], axis=2).astype(x.dtype)
    y_br = jnp.einsum('btnc,ncd->btnd', branches, w_br)
    gates = jax.nn.sigmoid(gate_logits.reshape(bsz, t_len, N_BRANCH, D_MODEL))
    x = x + jnp.sum(gates * y_br, axis=2) @ w_out
    h2 = rms_norm(x, n2_w)
    x = x + (jax.nn.silu(h2 @ w_gate) * (h2 @ w_up)) @ w_down
    new_state = (k_b.astype(past_k.dtype), v_b.astype(past_v.dtype), conv_new.astype(conv_hist.dtype),
                 s_gdn_new.astype(s_gdn.dtype), s_hgrn_new.astype(s_hgrn.dtype))
    return x, new_state


def setup_inputs(seed: int = 0) -> dict:
    key = jax.random.key(seed)
    ks = jax.random.split(key, 24)

    def nrm(k, shape, scale):
        return scale * jax.random.normal(k, shape, F32)

    dt = jnp.exp(jax.random.uniform(ks[12], (DEPTH, H_A), F32, math.log(1e-3), math.log(1e-1)))
    return {
        "x_prompt": nrm(ks[0], (BATCH, SEQ, D_MODEL), 1.0),
        "x_sample": nrm(ks[1], (DEC_BATCH, DEC_SEQ, D_MODEL), 1.0),
        "cache_sb_k": nrm(ks[2], (DEPTH, DEC_BATCH, PAST_LEN, H_B, DH_B), 1.0),
        "cache_sb_v": nrm(ks[3], (DEPTH, DEC_BATCH, PAST_LEN, H_B, DH_B), 1.0),
        "state_gdn_conv": nrm(ks[4], (DEPTH, DEC_BATCH, CONV_W - 1, GDN_CONV_DIM), 1.0),
        "state_gdn": nrm(ks[5], (DEPTH, DEC_BATCH, H_A, DK_A, DV_A), DK_A ** -0.5),
        "state_hgrn": nrm(ks[6], (DEPTH, DEC_BATCH, H_C, DK_C, DV_C), 1.0),
        "meta_tokens": nrm(ks[7], (N_META, D_MODEL), 1.0),
        "norm1_w": 1.0 + nrm(ks[8], (DEPTH, D_MODEL), 0.02),
        "w_in": nrm(ks[9], (DEPTH, D_MODEL, N_IN), D_MODEL ** -0.5),
        "gdn_conv_w": nrm(ks[10], (DEPTH, CONV_W, GDN_CONV_DIM), CONV_W ** -0.5),
        "gdn_a_log": jnp.log(jax.random.uniform(ks[11], (DEPTH, H_A), F32, 1.0, 16.0)),
        "gdn_dt_bias": dt + jnp.log(-jnp.expm1(-dt)),
        "gdn_norm_w": 1.0 + nrm(ks[13], (DEPTH, DV_A), 0.02),
        "hgrn_lb_param": nrm(ks[14], (DEPTH, HG_KEY), 0.5),
        "hgrn_norm_w": 1.0 + nrm(ks[15], (DEPTH, DV_C), 0.02),
        "w_branch": nrm(ks[16], (DEPTH, N_BRANCH, MIX_WIDTH, D_MODEL), MIX_WIDTH ** -0.5),
        "w_o": nrm(ks[17], (DEPTH, D_MODEL, D_MODEL), D_MODEL ** -0.5),
        "norm2_w": 1.0 + nrm(ks[18], (DEPTH, D_MODEL), 0.02),
        "w_ff_gate": nrm(ks[19], (DEPTH, D_MODEL, D_FF), D_MODEL ** -0.5),
        "w_ff_up": nrm(ks[20], (DEPTH, D_MODEL, D_FF), D_MODEL ** -0.5),
        "w_ff_down": nrm(ks[21], (DEPTH, D_FF, D_MODEL), D_FF ** -0.5),
        "final_norm_w": 1.0 + nrm(ks[22], (D_MODEL,), 0.02),
    }


def reference(x_prompt, x_sample, cache_sb_k, cache_sb_v, state_gdn_conv, state_gdn, state_hgrn,
              meta_tokens, norm1_w, w_in, gdn_conv_w, gdn_a_log, gdn_dt_bias, gdn_norm_w,
              hgrn_lb_param, hgrn_norm_w, w_branch, w_o, norm2_w, w_ff_gate, w_ff_up, w_ff_down,
              final_norm_w):
    lower_bounds = hgrn_lower_bounds(hgrn_lb_param)
    b_p = x_prompt.shape[0]
    meta = jnp.broadcast_to(meta_tokens.astype(x_prompt.dtype)[None], (b_p, N_META, D_MODEL))
    xp = jnp.concatenate([meta, x_prompt], axis=1)
    xs = x_sample
    empty_kv = jnp.zeros((b_p, 0, H_B, DH_B), cache_sb_k.dtype)
    zero_conv = jnp.zeros((b_p, CONV_W - 1, GDN_CONV_DIM), state_gdn_conv.dtype)
    zero_gdn = jnp.zeros((b_p, H_A, DK_A, DV_A), state_gdn.dtype)
    zero_hgrn = jnp.zeros((b_p, H_C, DK_C, DV_C), state_hgrn.dtype)
    st_p, st_s = [], []
    for l in range(DEPTH):
        weights = (lower_bounds[l], norm1_w[l], w_in[l], gdn_conv_w[l], gdn_a_log[l], gdn_dt_bias[l],
                   gdn_norm_w[l], hgrn_norm_w[l], w_branch[l], w_o[l], norm2_w[l],
                   w_ff_gate[l], w_ff_up[l], w_ff_down[l])
        xp, new_p = layer_step(xp, empty_kv, empty_kv, zero_conv, zero_gdn, zero_hgrn, *weights)
        xs, new_s = layer_step(xs, cache_sb_k[l], cache_sb_v[l], state_gdn_conv[l], state_gdn[l],
                               state_hgrn[l], *weights)
        st_p.append(new_p)
        st_s.append(new_s)
    y_prompt = rms_norm(xp[:, N_META:], final_norm_w)
    y_sample = rms_norm(xs, final_norm_w)
    sb_k_p = jnp.stack([s[0] for s in st_p])
    sb_v_p = jnp.stack([s[1] for s in st_p])
    conv_p = jnp.stack([s[2] for s in st_p])
    gdn_p = jnp.stack([s[3] for s in st_p])
    hgrn_p = jnp.stack([s[4] for s in st_p])
    sb_k_s = jnp.stack([s[0] for s in st_s])
    sb_v_s = jnp.stack([s[1] for s in st_s])
    conv_s = jnp.stack([s[2] for s in st_s])
    gdn_s = jnp.stack([s[3] for s in st_s])
    hgrn_s = jnp.stack([s[4] for s in st_s])
    return (y_prompt, y_sample, sb_k_p, sb_v_p, sb_k_s, sb_v_s, conv_p, conv_s, gdn_p, gdn_s, hgrn_p, hgrn_s)
```

```python
import functools

import jax
import jax.numpy as jnp
from jax import lax
from jax.experimental import pallas as pl
from jax.experimental.pallas import tpu as pltpu

F32 = jnp.float32
BF16 = jnp.bfloat16

D_MODEL = 4096
DEPTH = 4
N_HEAD = 16
HEAD_DIM = 128
MIX_WIDTH = N_HEAD * HEAD_DIM
N_META = 16
CONV_W = 4
D_FF = 11008
RMS_EPS = 1e-6
N_BRANCH = 3

LANES = 128
SUBLANES = 8
VMEM_LIMIT_CAP = 60 * 1024 * 1024

CHUNK_P = 64
D_FF_PAD = -(-D_FF // 1024) * 1024
N_MIX = 11 * MIX_WIDTH
N_GATE = N_BRANCH * D_MODEL
SEC_GQ, SEC_GK, SEC_GV, SEC_GZ, SEC_SQ, SEC_SK, SEC_SV, SEC_HQ, SEC_HF, SEC_HI, SEC_HG = range(11)
_W_IN_OFF = {
    SEC_GQ: 0, SEC_GK: 2048, SEC_GV: 4096, SEC_GZ: 6144,
    SEC_SQ: 8224, SEC_SK: 10272, SEC_SV: 12320,
    SEC_HQ: 14368, SEC_HF: 16416, SEC_HI: 18464, SEC_HG: 20512,
}
_W_IN_AB = 8192
_W_IN_GATE = 22560


def _vmem_limit(*block_bytes):
    need = 2 * sum(block_bytes) + 2 * max(block_bytes)
    return int(min(max(need, 16 * 1024 * 1024), VMEM_LIMIT_CAP))


def _sigmoid(x):
    return 1.0 / (1.0 + jnp.exp(-x))


def _silu(x):
    return x * _sigmoid(x)


def _softplus(x):
    return jnp.maximum(x, 0.0) + jnp.log1p(jnp.exp(-jnp.abs(x)))


def _dot(a, b):
    return jnp.dot(a.astype(BF16), b.astype(BF16), preferred_element_type=F32)


def _dot_nt(a, b):
    return lax.dot_general(a.astype(BF16), b.astype(BF16), (((1,), (1,)), ((), ())),
                           preferred_element_type=F32)


def _dot_tn(a, b):
    return lax.dot_general(a.astype(BF16), b.astype(BF16), (((0,), (0,)), ((), ())),
                           preferred_element_type=F32)


def _split(x):
    hi = x.astype(BF16)
    lo = (x - hi.astype(F32)).astype(BF16)
    return hi, lo


def _dot_hp(a, b):
    a_hi, a_lo = _split(a)
    b_hi, b_lo = _split(b)
    d = functools.partial(jnp.dot, preferred_element_type=F32)
    return d(a_hi, b_hi) + (d(a_hi, b_lo) + d(a_lo, b_hi))


def _dot_exact_lhs(sel, x):
    hi, lo = _split(x)
    d = functools.partial(jnp.dot, preferred_element_type=F32)
    return d(sel, hi) + d(sel, lo)


def _rmsnorm_kernel(x_ref, w_ref, o_ref):
    x = x_ref[...]
    ms = jnp.mean(x * x, axis=-1, keepdims=True)
    o_ref[...] = (x * lax.rsqrt(ms + RMS_EPS) * w_ref[...]).astype(o_ref.dtype)


def _row_tile(m, cap):
    return max(t for t in range(16, cap + 1, 16) if m % t == 0)


def _rmsnorm(x, w, out_dtype):
    m, d = x.shape
    tm = _row_tile(m, 256)
    return pl.pallas_call(
        _rmsnorm_kernel,
        grid=(m // tm,),
        in_specs=[pl.BlockSpec((tm, d), lambda i: (i, 0)),
                  pl.BlockSpec((1, d), lambda i: (0, 0))],
        out_specs=pl.BlockSpec((tm, d), lambda i: (i, 0)),
        out_shape=jax.ShapeDtypeStruct((m, d), out_dtype),
        compiler_params=pltpu.CompilerParams(
            dimension_semantics=("arbitrary",),
            vmem_limit_bytes=_vmem_limit(tm * d * 4, tm * d * 4)),
        name="rmsnorm",
    )(x, w.reshape(1, d))


def _mm_kernel(a_ref, w_ref, o_ref):
    o_ref[...] = jnp.dot(a_ref[...], w_ref[...], preferred_element_type=F32).astype(o_ref.dtype)


def _mm_headmajor_kernel(a_ref, w_ref, o_ref):
    y = jnp.dot(a_ref[...], w_ref[...], preferred_element_type=F32)
    for j in range(o_ref.shape[0]):
        o_ref[j] = y[:, j * LANES:(j + 1) * LANES].astype(o_ref.dtype)


def _mm_sigmoid_kernel(a_ref, w_ref, o_ref):
    y = jnp.dot(a_ref[...], w_ref[...], preferred_element_type=F32)
    o_ref[...] = _sigmoid(y).astype(o_ref.dtype)


def _mm_residual_kernel(a_ref, w_ref, r_ref, o_ref):
    y = jnp.dot(a_ref[...], w_ref[...], preferred_element_type=F32)
    o_ref[...] = r_ref[...] + y


def _mm_swiglu_kernel(a_ref, wg_ref, wu_ref, o_ref):
    a = a_ref[...]
    g = jnp.dot(a, wg_ref[...], preferred_element_type=F32)
    u = jnp.dot(a, wu_ref[...], preferred_element_type=F32)
    o_ref[...] = (_silu(g) * u).astype(o_ref.dtype)


def _dense(a, w, *, tm, tn, mode="plain", out_dtype=F32, residual=None, w2=None, name="dense"):
    m, k = a.shape
    n = w.shape[1]
    grid = (m // tm, n // tn)
    a_spec = pl.BlockSpec((tm, k), lambda i, j: (i, 0))
    w_spec = pl.BlockSpec((k, tn), lambda i, j: (0, j))
    o_spec = pl.BlockSpec((tm, tn), lambda i, j: (i, j))
    out_shape = jax.ShapeDtypeStruct((m, n), out_dtype)
    out_bytes = tm * tn * jnp.dtype(out_dtype).itemsize
    blocks = [tm * k * 2, k * tn * 2, out_bytes, tm * tn * 4]
    if mode == "plain":
        kern, ins, specs = _mm_kernel, (a, w), [a_spec, w_spec]
    elif mode == "headmajor":
        kern, ins, specs = _mm_headmajor_kernel, (a, w), [a_spec, w_spec]
        o_spec = pl.BlockSpec((tn // LANES, tm, LANES), lambda i, j: (j, i, 0))
        out_shape = jax.ShapeDtypeStruct((n // LANES, m, LANES), out_dtype)
    elif mode == "sigmoid":
        kern, ins, specs = _mm_sigmoid_kernel, (a, w), [a_spec, w_spec]
    elif mode == "residual":
        kern, ins, specs = _mm_residual_kernel, (a, w, residual), [a_spec, w_spec, o_spec]
        blocks.append(tm * tn * 4)
    elif mode == "swiglu":
        kern, ins, specs = _mm_swiglu_kernel, (a, w, w2), [a_spec, w_spec, w_spec]
        blocks += [k * tn * 2, tm * tn * 4]
    else:
        raise ValueError(mode)
    return pl.pallas_call(
        kern, grid=grid, in_specs=specs, out_specs=o_spec, out_shape=out_shape,
        compiler_params=pltpu.CompilerParams(
            dimension_semantics=("arbitrary", "arbitrary"),
            vmem_limit_bytes=_vmem_limit(*blocks)),
        name=name,
    )(*ins)


def _mm_ktiled_kernel(a_ref, w_ref, r_ref, o_ref, acc_ref):
    kk = pl.program_id(2)
    y = jnp.dot(a_ref[...], w_ref[...], preferred_element_type=F32)

    @pl.when(kk == 0)
    def _():
        acc_ref[...] = r_ref[...] + y

    @pl.when(kk > 0)
    def _():
        acc_ref[...] += y

    @pl.when(kk == pl.num_programs(2) - 1)
    def _():
        o_ref[...] = acc_ref[...]


def _dense_ktiled_residual(a, w, residual, *, tm, tn, tk, name):
    m, k = a.shape
    n = w.shape[1]
    return pl.pallas_call(
        _mm_ktiled_kernel,
        grid=(m // tm, n // tn, k // tk),
        in_specs=[pl.BlockSpec((tm, tk), lambda i, j, kk: (i, kk)),
                  pl.BlockSpec((tk, tn), lambda i, j, kk: (kk, j)),
                  pl.BlockSpec((tm, tn), lambda i, j, kk: (i, j))],
        out_specs=pl.BlockSpec((tm, tn), lambda i, j, kk: (i, j)),
        out_shape=jax.ShapeDtypeStruct((m, n), F32),
        scratch_shapes=[pltpu.VMEM((tm, tn), F32)],
        compiler_params=pltpu.CompilerParams(
            dimension_semantics=("arbitrary", "arbitrary", "arbitrary"),
            vmem_limit_bytes=_vmem_limit(tm * tk * 2, tk * tn * 2, tm * tn * 4, tm * tn * 4,
                                         tm * tn * 4)),
        name=name,
    )(a, w, residual)


def _merge_kernel(o_ref, w_ref, g_ref, out_ref, acc_ref):
    br = pl.program_id(2)
    a = jnp.concatenate([o_ref[0, h] for h in range(N_HEAD)], axis=1)
    y = jnp.dot(a, w_ref[0], preferred_element_type=F32)
    contrib = g_ref[...].astype(F32) * y

    @pl.when(br == 0)
    def _():
        acc_ref[...] = contrib

    @pl.when(br > 0)
    def _():
        acc_ref[...] += contrib

    @pl.when(br == N_BRANCH - 1)
    def _():
        out_ref[...] = acc_ref[...].astype(out_ref.dtype)


def _merge(branches, w_br, gates, *, tm, tn):
    m = branches.shape[2]
    n_col = D_MODEL // tn
    return pl.pallas_call(
        _merge_kernel,
        grid=(m // tm, n_col, N_BRANCH),
        in_specs=[pl.BlockSpec((1, N_HEAD, tm, LANES), lambda i, j, g: (g, 0, i, 0)),
                  pl.BlockSpec((1, MIX_WIDTH, tn), lambda i, j, g: (g, 0, j)),
                  pl.BlockSpec((tm, tn), lambda i, j, g: (i, g * n_col + j))],
        out_specs=pl.BlockSpec((tm, tn), lambda i, j, g: (i, j)),
        out_shape=jax.ShapeDtypeStruct((m, D_MODEL), BF16),
        scratch_shapes=[pltpu.VMEM((tm, tn), F32)],
        compiler_params=pltpu.CompilerParams(
            dimension_semantics=("arbitrary", "arbitrary", "arbitrary"),
            vmem_limit_bytes=_vmem_limit(tm * MIX_WIDTH * 2, MIX_WIDTH * tn * 2, tm * tn * 2,
                                         tm * tn * 2, tm * tn * 4, tm * MIX_WIDTH * 2)),
        name="branch_merge",
    )(branches, w_br, gates)


def _lower_bound_kernel(p_ref, o_ref):
    p = p_ref[...]
    e = jnp.exp(p - jnp.max(p, axis=0, keepdims=True))
    s = e / jnp.sum(e, axis=0, keepdims=True)
    run = jnp.zeros_like(s[0:1])
    for l in range(DEPTH):
        run = run + s[l:l + 1]
        o_ref[l:l + 1, :] = run - s[0:1]


def _lower_bounds(p):
    return pl.pallas_call(
        _lower_bound_kernel,
        out_shape=jax.ShapeDtypeStruct(p.shape, F32),
        name="hgrn_lower_bounds",
    )(p)


def _gdn_kernel(q_ref, k_ref, v_ref, z_ref, ab_ref, hist_ref, s0_ref, cw_ref, hp_ref, nw_ref,
                o_ref, sout_ref, prev_ref, s_ref, *, chunk, t_valid):
    c = pl.program_id(1)
    C = chunk

    @pl.when(c == 0)
    def _():
        prev_ref[...] = jnp.zeros_like(prev_ref)
        prev_ref[:, :, C - SUBLANES:, :] = hist_ref[0]
        s_ref[...] = s0_ref[0]

    row = lax.broadcasted_iota(jnp.int32, (C, LANES), 0)
    lane = lax.broadcasted_iota(jnp.int32, (C, LANES), 1)
    valid = (c * C + row) < t_valid
    valid_col = (c * C + lax.broadcasted_iota(jnp.int32, (C, 1), 0)) < t_valid
    ii = lax.broadcasted_iota(jnp.int32, (C, C), 0)
    jj = lax.broadcasted_iota(jnp.int32, (C, C), 1)
    eye = ii == jj
    tri = jj <= ii
    strict = jj < ii
    same16 = (ii >> 4) == (jj >> 4)
    eye_f = jnp.where(eye, 1.0, 0.0).astype(F32)

    ab = ab_ref[...]
    g_all = -jnp.exp(hp_ref[0:1, :]) * _softplus(ab + hp_ref[1:2, :])
    beta_all = _sigmoid(ab)
    nw = nw_ref[...]

    def conv(p, x_ref, h):
        x = x_ref[h]
        xp = prev_ref[p, h]
        w = cw_ref[p, h]
        y = x * w[CONV_W - 1:CONV_W, :]
        for j in range(1, CONV_W):
            xs = jnp.where(row < j, pltpu.roll(xp, j, 0), pltpu.roll(x, j, 0))
            y = y + xs * w[CONV_W - 1 - j:CONV_W - j, :]
        prev_ref[p, h] = x
        return _silu(y)

    def l2n(x):
        return x * lax.rsqrt(jnp.sum(x * x, axis=-1, keepdims=True) + RMS_EPS)

    def head(h, carry):
        q = l2n(conv(0, q_ref, h)) * (HEAD_DIM ** -0.5)
        k = jnp.where(valid, l2n(conv(1, k_ref, h)), 0.0)
        v = conv(2, v_ref, h)
        g_col = jnp.sum(jnp.where(lane == h, g_all, 0.0), axis=1, keepdims=True)
        b_col = jnp.sum(jnp.where(lane == h + N_HEAD, beta_all, 0.0), axis=1, keepdims=True)
        g_col = jnp.where(valid_col, g_col, 0.0)
        b_col = jnp.where(valid_col, b_col, 0.0)

        g_row = jnp.sum(jnp.where(eye, g_col, 0.0), axis=0, keepdims=True)
        gc_col = jnp.sum(jnp.where(tri, g_row, 0.0), axis=1, keepdims=True)
        gc_row = jnp.sum(jnp.where(eye, gc_col, 0.0), axis=0, keepdims=True)
        diff = gc_col - gc_row
        decay = jnp.where(tri, jnp.exp(jnp.where(tri, diff, 0.0)), 0.0)
        egc = jnp.exp(gc_col)
        g_last = gc_col[C - 1:C, :]

        kb = k * b_col
        qk_kbk = _dot_nt(jnp.concatenate([q, kb], axis=0), k)
        attn = jnp.where(tri, qk_kbk[:C] * decay, 0.0)
        lower = jnp.where(strict, qk_kbk[C:] * decay, 0.0)

        rhs = jnp.concatenate([v * b_col, kb * egc], axis=1)
        ld = jnp.where(same16, lower, 0.0)
        lo = lower - ld
        p2 = _dot_hp(ld, ld)
        p4 = _dot_hp(p2, p2)
        p8 = _dot_hp(p4, p4)
        td = _dot_hp(_dot_hp(_dot_hp(eye_f - ld, eye_f + p2), eye_f + p4), eye_f + p8)
        y = _dot_hp(td, rhs)
        nm = _dot_hp(td, lo)
        n2 = _dot_hp(nm, nm)
        zz = y + _dot_hp(n2, y)
        sol = zz - _dot_hp(nm, zz)
        u = sol[:, :HEAD_DIM]
        w = sol[:, HEAD_DIM:]

        s = s_ref[h]
        ws_qs = _dot(jnp.concatenate([w, q * egc], axis=0), s)
        v_new = u - ws_qs[:C]
        o = ws_qs[C:] + _dot(attn, v_new)
        s_ref[h] = s * jnp.exp(g_last) + _dot_tn(k * jnp.exp(g_last - gc_col), v_new)

        o = o * lax.rsqrt(jnp.mean(o * o, axis=-1, keepdims=True) + RMS_EPS) * nw
        o_ref[h] = (o * _silu(z_ref[h])).astype(o_ref.dtype)
        return carry

    lax.fori_loop(0, N_HEAD, head, 0)

    @pl.when(c == pl.num_programs(1) - 1)
    def _():
        sout_ref[0] = s_ref[...]


def _gdn(proj, ab, hist, s0, cw, hp, nw, *, batch, row0, t_pad, chunk, t_valid):
    m = proj.shape[1]
    nc = t_pad // chunk
    rb0 = row0 // chunk

    def sec(s):
        return pl.BlockSpec((N_HEAD, chunk, LANES), lambda b, c: (s, rb0 + b * nc + c, 0))

    kern = functools.partial(_gdn_kernel, chunk=chunk, t_valid=t_valid)
    o, s_out = pl.pallas_call(
        kern,
        grid=(batch, nc),
        in_specs=[sec(SEC_GQ), sec(SEC_GK), sec(SEC_GV), sec(SEC_GZ),
                  pl.BlockSpec((chunk, LANES), lambda b, c: (rb0 + b * nc + c, 0)),
                  pl.BlockSpec((1, 3, N_HEAD, SUBLANES, LANES), lambda b, c: (b, 0, 0, 0, 0)),
                  pl.BlockSpec((1, N_HEAD, HEAD_DIM, HEAD_DIM), lambda b, c: (b, 0, 0, 0)),
                  pl.BlockSpec((3, N_HEAD, SUBLANES, LANES), lambda b, c: (0, 0, 0, 0)),
                  pl.BlockSpec((SUBLANES, LANES), lambda b, c: (0, 0)),
                  pl.BlockSpec((1, LANES), lambda b, c: (0, 0))],
        out_specs=[pl.BlockSpec((N_HEAD, chunk, LANES), lambda b, c: (0, b * nc + c, 0)),
                   pl.BlockSpec((1, N_HEAD, HEAD_DIM, HEAD_DIM), lambda b, c: (b, 0, 0, 0))],
        out_shape=[jax.ShapeDtypeStruct((N_HEAD, batch * t_pad, LANES), BF16),
                   jax.ShapeDtypeStruct((batch, N_HEAD, HEAD_DIM, HEAD_DIM), F32)],
        scratch_shapes=[pltpu.VMEM((3, N_HEAD, chunk, LANES), F32),
                        pltpu.VMEM((N_HEAD, HEAD_DIM, HEAD_DIM), F32)],
        compiler_params=pltpu.CompilerParams(
            dimension_semantics=("arbitrary", "arbitrary"),
            vmem_limit_bytes=32 * 1024 * 1024),
        name="gdn_mixer",
    )(proj, proj, proj, proj, ab, hist, s0, cw, hp, nw)
    del m
    return o, s_out


def _hgrn_kernel(q_ref, f_ref, i_ref, g_ref, lb_ref, s0_ref, nw_ref, o_ref, sout_ref, s_ref,
                 *, chunk, t_valid):
    c = pl.program_id(1)
    C = chunk
    halves = [r for r in (32, 16, 8) if 2 * r <= C]

    @pl.when(c == 0)
    def _():
        s_ref[...] = s0_ref[0]

    row = lax.broadcasted_iota(jnp.int32, (C, LANES), 0)
    valid = (c * C + row) < t_valid
    ii = lax.broadcasted_iota(jnp.int32, (C, C), 0)
    jj = lax.broadcasted_iota(jnp.int32, (C, C), 1)
    sel_blocks = [jnp.where(jj <= ii, 1.0, 0.0)]
    level_masks = []
    for r in halves:
        ref_row = (ii & ~(2 * r - 1)) + (r - 1)
        sel_blocks.append(jnp.where(jj <= ref_row, 1.0, 0.0))
        same_pair = (ii & ~(2 * r - 1)) == (jj & ~(2 * r - 1))
        level_masks.append(same_pair & ((ii & r) != 0) & ((jj & r) == 0))
    sel = jnp.concatenate(sel_blocks, axis=0).astype(BF16)
    nw = nw_ref[...]
    row8 = lax.broadcasted_iota(jnp.int32, (SUBLANES, LANES), 0)

    def head(h, carry):
        lb = lb_ref[h]
        q = _silu(q_ref[h])
        f = lb + (1.0 - lb) * _sigmoid(f_ref[h])
        logf = jnp.where(valid, jnp.log(f), 0.0)
        k = jnp.where(valid, 1.0 - f, 0.0)
        v = i_ref[h]

        cums = _dot_exact_lhs(sel, logf)
        b = cums[:C]

        attn = jnp.zeros((C, C), F32)
        for l, mask in enumerate(level_masks):
            bref = cums[(l + 1) * C:(l + 2) * C]
            qt = q * jnp.exp(jnp.minimum(b - bref, 0.0))
            kt = k * jnp.exp(jnp.minimum(bref - b, 0.0))
            attn = attn + jnp.where(mask, _dot_nt(qt, kt), 0.0)
        o = _dot(attn, v)

        diag = []
        for d in range(C // SUBLANES):
            sl = slice(d * SUBLANES, (d + 1) * SUBLANES)
            qd, kd, bd, vd = q[sl], k[sl], b[sl], v[sl]
            od = jnp.zeros((SUBLANES, LANES), F32)
            for s in range(SUBLANES):
                e = jnp.exp(jnp.minimum(bd - bd[s:s + 1], 0.0))
                wgt = jnp.sum(qd * kd[s:s + 1] * e, axis=1, keepdims=True)
                od = od + jnp.where(row8 >= s, wgt, 0.0) * vd[s:s + 1]
            diag.append(od)
        o = o + jnp.concatenate(diag, axis=0)

        st = s_ref[h]
        o = o + _dot_nt(q * jnp.exp(b), st)
        b_last = b[C - 1:C]
        s_ref[h] = st * jnp.exp(b_last) + _dot_tn(v, k * jnp.exp(b_last - b))

        o = o * lax.rsqrt(jnp.mean(o * o, axis=-1, keepdims=True) + RMS_EPS) * nw
        o_ref[h] = (o * _silu(g_ref[h])).astype(o_ref.dtype)
        return carry

    lax.fori_loop(0, N_HEAD, head, 0)

    @pl.when(c == pl.num_programs(1) - 1)
    def _():
        sout_ref[0] = s_ref[...]


def _hgrn(proj, lb, s0_t, nw, *, batch, row0, t_pad, chunk, t_valid):
    nc = t_pad // chunk
    rb0 = row0 // chunk

    def sec(s):
        return pl.BlockSpec((N_HEAD, chunk, LANES), lambda b, c: (s, rb0 + b * nc + c, 0))

    kern = functools.partial(_hgrn_kernel, chunk=chunk, t_valid=t_valid)
    return pl.pallas_call(
        kern,
        grid=(batch, nc),
        in_specs=[sec(SEC_HQ), sec(SEC_HF), sec(SEC_HI), sec(SEC_HG),
                  pl.BlockSpec((N_HEAD, 1, LANES), lambda b, c: (0, 0, 0)),
                  pl.BlockSpec((1, N_HEAD, HEAD_DIM, HEAD_DIM), lambda b, c: (b, 0, 0, 0)),
                  pl.BlockSpec((1, LANES), lambda b, c: (0, 0))],
        out_specs=[pl.BlockSpec((N_HEAD, chunk, LANES), lambda b, c: (0, b * nc + c, 0)),
                   pl.BlockSpec((1, N_HEAD, HEAD_DIM, HEAD_DIM), lambda b, c: (b, 0, 0, 0))],
        out_shape=[jax.ShapeDtypeStruct((N_HEAD, batch * t_pad, LANES), BF16),
                   jax.ShapeDtypeStruct((batch, N_HEAD, HEAD_DIM, HEAD_DIM), F32)],
        scratch_shapes=[pltpu.VMEM((N_HEAD, HEAD_DIM, HEAD_DIM), F32)],
        compiler_params=pltpu.CompilerParams(
            dimension_semantics=("arbitrary", "arbitrary"),
            vmem_limit_bytes=32 * 1024 * 1024),
        name="hgrn_mixer",
    )(proj, proj, proj, proj, lb, s0_t, nw)


def _sb_kernel(*refs, q_block, t_new, t_past):
    if t_past:
        q_ref, kn_ref, vn_ref, kp_ref, vp_ref, o_ref, kb_ref, vb_ref = refs
    else:
        q_ref, kn_ref, vn_ref, o_ref, kb_ref, vb_ref = refs
    qi = pl.program_id(2)
    t_all = t_past + t_new
    t_scr = kb_ref.shape[0]

    @pl.when(qi == 0)
    def _():
        if t_past:
            kb_ref[0:t_past, :] = kp_ref[0].astype(BF16)
            vb_ref[0:t_past, :] = vp_ref[0].astype(BF16)
        kb_ref[t_past:t_all, :] = kn_ref[0].astype(BF16)
        vb_ref[t_past:t_all, :] = vn_ref[0].astype(BF16)
        if t_scr > t_all:
            kb_ref[t_all:, :] = jnp.zeros((t_scr - t_all, LANES), BF16)
            vb_ref[t_all:, :] = jnp.zeros((t_scr - t_all, LANES), BF16)

    q = q_ref[0].astype(BF16)
    q_pos = t_past + qi * q_block + lax.broadcasted_iota(jnp.int32, (q_block, LANES), 0)
    lane = lax.broadcasted_iota(jnp.int32, (q_block, LANES), 1)
    jr = lax.broadcasted_iota(jnp.int32, (LANES, 2 * LANES), 0)
    sc = lax.broadcasted_iota(jnp.int32, (LANES, 2 * LANES), 1)
    suffix_and_total = jnp.where((sc >= LANES) | (jr > sc), 1.0, 0.0).astype(BF16)
    kb_hi = (t_past + qi * q_block + q_block - 1) // LANES
    scale = HEAD_DIM ** -0.5

    def body(it, carry):
        run, acc = carry
        kb = kb_hi - it
        start = pl.multiple_of(kb * LANES, LANES)
        kt = kb_ref[pl.ds(start, LANES), :]
        vt = vb_ref[pl.ds(start, LANES), :]
        z = lax.dot_general(q, kt, (((1,), (1,)), ((), ())), preferred_element_type=F32) * scale
        vis = (kb * LANES + lane) < q_pos
        sp = _softplus(z)
        log_keep = jnp.where(vis, -sp, 0.0)
        sums = _dot_exact_lhs_rhs(log_keep, suffix_and_total)
        between = sums[:, :LANES] + run
        a = jnp.where(vis, jnp.exp((z - sp) + between), 0.0)
        acc = acc + jnp.dot(a.astype(BF16), vt, preferred_element_type=F32)
        return run + sums[:, LANES:], acc

    zero = jnp.zeros((q_block, LANES), F32)
    _, acc = lax.fori_loop(0, kb_hi + 1, body, (zero, zero))
    o_ref[0] = acc.astype(o_ref.dtype)


def _dot_exact_lhs_rhs(x, sel):
    hi, lo = _split(x)
    d = functools.partial(jnp.dot, preferred_element_type=F32)
    return d(hi, sel) + d(lo, sel)


def _sb(proj, past_k, past_v, *, batch, row0, t_pad, q_block, t_past):
    nq = t_pad // q_block
    rbq0 = row0 // q_block
    rbk0 = row0 // t_pad
    t_scr = -(-(t_past + t_pad) // LANES) * LANES
    sec_q = SEC_SQ * N_HEAD
    sec_k = SEC_SK * N_HEAD
    sec_v = SEC_SV * N_HEAD
    in_specs = [pl.BlockSpec((1, q_block, LANES), lambda b, h, i: (sec_q + h, rbq0 + b * nq + i, 0)),
                pl.BlockSpec((1, t_pad, LANES), lambda b, h, i: (sec_k + h, rbk0 + b, 0)),
                pl.BlockSpec((1, t_pad, LANES), lambda b, h, i: (sec_v + h, rbk0 + b, 0))]
    args = [proj, proj, proj]
    if t_past:
        in_specs += [pl.BlockSpec((1, t_past, LANES), lambda b, h, i: (b, 0, h)),
                     pl.BlockSpec((1, t_past, LANES), lambda b, h, i: (b, 0, h))]
        args += [past_k, past_v]
    kern = functools.partial(_sb_kernel, q_block=q_block, t_new=t_pad, t_past=t_past)
    return pl.pallas_call(
        kern,
        grid=(batch, N_HEAD, nq),
        in_specs=in_specs,
        out_specs=pl.BlockSpec((1, q_block, LANES), lambda b, h, i: (h, b * nq + i, 0)),
        out_shape=jax.ShapeDtypeStruct((N_HEAD, batch * t_pad, LANES), BF16),
        scratch_shapes=[pltpu.VMEM((t_scr, LANES), BF16), pltpu.VMEM((t_scr, LANES), BF16)],
        compiler_params=pltpu.CompilerParams(
            dimension_semantics=("arbitrary", "arbitrary", "arbitrary"),
            vmem_limit_bytes=32 * 1024 * 1024),
        name="stickbreak_attention",
    )(*args)


def kernel(x_prompt, x_sample, cache_sb_k, cache_sb_v, state_gdn_conv, state_gdn, state_hgrn,
           meta_tokens, norm1_w, w_in, gdn_conv_w, gdn_a_log, gdn_dt_bias, gdn_norm_w,
           hgrn_lb_param, hgrn_norm_w, w_branch, w_o, norm2_w, w_ff_gate, w_ff_up, w_ff_down,
           final_norm_w):
    b_p, seq, _ = x_prompt.shape
    b_s, t_s, _ = x_sample.shape
    t_p = N_META + seq
    t_pp = -(-t_p // CHUNK_P) * CHUNK_P
    m_p = b_p * t_pp
    m_s = b_s * t_s
    m_all = m_p + m_s
    past_len = cache_sb_k.shape[2]
    tm = _row_tile(m_all, 1088)

    meta = jnp.broadcast_to(meta_tokens[None], (b_p, N_META, D_MODEL)).astype(F32)
    xp = jnp.concatenate([meta, x_prompt, jnp.zeros((b_p, t_pp - t_p, D_MODEL), F32)], axis=1)
    x = jnp.concatenate([xp.reshape(m_p, D_MODEL), x_sample.reshape(m_s, D_MODEL)], axis=0)

    lower_bounds = _lower_bounds(hgrn_lb_param)

    zero_hist = jnp.zeros((b_p, 3, N_HEAD, SUBLANES, LANES), F32)
    zero_state = jnp.zeros((b_p, N_HEAD, HEAD_DIM, HEAD_DIM), F32)
    past_k2 = cache_sb_k.reshape(DEPTH, b_s, past_len, MIX_WIDTH)
    past_v2 = cache_sb_v.reshape(DEPTH, b_s, past_len, MIX_WIDTH)

    outs = {name: [] for name in ("kp", "vp", "ks", "vs", "cp", "cs", "gp", "gs", "hp", "hs")}
    for l in range(DEPTH):
        wl = w_in[l]
        w_mix = jnp.concatenate(
            [wl[:, _W_IN_OFF[s]:_W_IN_OFF[s] + MIX_WIDTH] for s in range(11)], axis=1).astype(BF16)
        w_ab = jnp.pad(wl[:, _W_IN_AB:_W_IN_AB + 2 * N_HEAD], ((0, 0), (0, LANES - 2 * N_HEAD))).astype(BF16)
        w_gate_in = wl[:, _W_IN_GATE:].astype(BF16)

        h = _rmsnorm(x, norm1_w[l], BF16)
        proj = _dense(h, w_mix, tm=tm, tn=1024, mode="headmajor", name="proj_mixers")
        ab = _dense(h, w_ab, tm=tm, tn=LANES, name="proj_ab")
        gates = _dense(h, w_gate_in, tm=tm, tn=1024, mode="sigmoid", out_dtype=BF16, name="proj_gates")

        cw = gdn_conv_w[l].reshape(CONV_W, 3, N_HEAD, HEAD_DIM).transpose(1, 2, 0, 3)
        cw = jnp.pad(cw, ((0, 0), (0, 0), (0, SUBLANES - CONV_W), (0, 0)))
        hp = jnp.zeros((SUBLANES, LANES), F32)
        hp = hp.at[0, :N_HEAD].set(gdn_a_log[l]).at[1, :N_HEAD].set(gdn_dt_bias[l])
        gnw = gdn_norm_w[l].reshape(1, HEAD_DIM)
        hist_s = state_gdn_conv[l].reshape(b_s, CONV_W - 1, 3, N_HEAD, HEAD_DIM).transpose(0, 2, 3, 1, 4)
        hist_s = jnp.pad(hist_s, ((0, 0), (0, 0), (0, 0), (SUBLANES - (CONV_W - 1), 0), (0, 0)))
        oa_p, gs_p = _gdn(proj, ab, zero_hist, zero_state, cw, hp, gnw,
                          batch=b_p, row0=0, t_pad=t_pp, chunk=CHUNK_P, t_valid=t_p)
        oa_s, gs_s = _gdn(proj, ab, hist_s, state_gdn[l], cw, hp, gnw,
                          batch=b_s, row0=m_p, t_pad=t_s, chunk=t_s, t_valid=t_s)

        ob_p = _sb(proj, None, None, batch=b_p, row0=0, t_pad=t_pp, q_block=CHUNK_P, t_past=0)
        ob_s = _sb(proj, past_k2[l], past_v2[l], batch=b_s, row0=m_p, t_pad=t_s, q_block=t_s,
                   t_past=past_len)

        lb = lower_bounds[l].reshape(N_HEAD, 1, HEAD_DIM)
        hnw = hgrn_norm_w[l].reshape(1, HEAD_DIM)
        oc_p, hs_p = _hgrn(proj, lb, zero_state, hnw, batch=b_p, row0=0, t_pad=t_pp,
                           chunk=CHUNK_P, t_valid=t_p)
        oc_s, hs_s = _hgrn(proj, lb, state_hgrn[l].swapaxes(-1, -2), hnw, batch=b_s, row0=m_p,
                           t_pad=t_s, chunk=t_s, t_valid=t_s)

        branches = jnp.stack([jnp.concatenate([oa_p, oa_s], axis=1),
                              jnp.concatenate([ob_p, ob_s], axis=1),
                              jnp.concatenate([oc_p, oc_s], axis=1)])
        merged = _merge(branches, w_branch[l].astype(BF16), gates, tm=tm, tn=1024)
        x = _dense(merged, w_o[l].astype(BF16), tm=tm, tn=1024, mode="residual", residual=x,
                   name="out_proj")

        h2 = _rmsnorm(x, norm2_w[l], BF16)
        pad_ff = ((0, 0), (0, D_FF_PAD - D_FF))
        hmid = _dense(h2, jnp.pad(w_ff_gate[l], pad_ff).astype(BF16), tm=tm, tn=512, mode="swiglu",
                      out_dtype=BF16, w2=jnp.pad(w_ff_up[l], pad_ff).astype(BF16), name="ffn_gate_up")
        w_down = jnp.pad(w_ff_down[l], ((0, D_FF_PAD - D_FF), (0, 0))).astype(BF16)
        x = _dense_ktiled_residual(hmid, w_down, x, tm=tm, tn=1024, tk=1024, name="ffn_down")

        def rows(sec0, nsec, r0, nb, tp):
            blk = proj[sec0 * N_HEAD:(sec0 + nsec) * N_HEAD, r0:r0 + nb * tp]
            return blk.reshape(nsec * N_HEAD, nb, tp, HEAD_DIM).transpose(1, 2, 0, 3)

        outs["kp"].append(rows(SEC_SK, 1, 0, b_p, t_pp)[:, :t_p])
        outs["vp"].append(rows(SEC_SV, 1, 0, b_p, t_pp)[:, :t_p])
        outs["ks"].append(rows(SEC_SK, 1, m_p, b_s, t_s))
        outs["vs"].append(rows(SEC_SV, 1, m_p, b_s, t_s))
        qkv_p = proj[:3 * N_HEAD, :m_p].reshape(3 * N_HEAD, b_p, t_pp, HEAD_DIM)[:, :, t_p - 3:t_p]
        outs["cp"].append(qkv_p.transpose(1, 2, 0, 3).reshape(b_p, CONV_W - 1, 3 * MIX_WIDTH))
        qkv_s = proj[:3 * N_HEAD, m_p:].reshape(3 * N_HEAD, b_s, t_s, HEAD_DIM)[:, :, t_s - 3:]
        outs["cs"].append(qkv_s.transpose(1, 2, 0, 3).reshape(b_s, CONV_W - 1, 3 * MIX_WIDTH))
        outs["gp"].append(gs_p)
        outs["gs"].append(gs_s)
        outs["hp"].append(hs_p.swapaxes(-1, -2))
        outs["hs"].append(hs_s.swapaxes(-1, -2))

    y = _rmsnorm(x, final_norm_w, F32)
    y_prompt = y[:m_p].reshape(b_p, t_pp, D_MODEL)[:, N_META:t_p]
    y_sample = y[m_p:].reshape(b_s, t_s, D_MODEL)
    st = {k: jnp.stack(v) for k, v in outs.items()}
    return (y_prompt, y_sample, st["kp"], st["vp"], st["ks"], st["vs"], st["cp"], st["cs"],
            st["gp"], st["gs"], st["hp"], st["hs"])
```

```python
import functools

import jax
import jax.numpy as jnp
from jax import lax
from jax.experimental import pallas as pl
from jax.experimental.pallas import tpu as pltpu

F32 = jnp.float32
BF16 = jnp.bfloat16

D_MODEL = 4096
DEPTH = 4
N_HEAD = 16
HEAD_DIM = 128
MIX_WIDTH = N_HEAD * HEAD_DIM
N_META = 16
CONV_W = 4
D_FF = 11008
RMS_EPS = 1e-6
N_BRANCH = 3

LANES = 128
SUBLANES = 8
VMEM_LIMIT_CAP = 60 * 1024 * 1024

CHUNK_P = 64
SB_Q_BLOCK_P = 192
SB_KEY_SLAB = 3 * LANES
D_FF_PAD = -(-D_FF // 1024) * 1024
GDN_HEADS_PER_STEP = 8
HGRN_HEADS_PER_STEP = 4
EXP_UNDERFLOW = -104.0
SEC_GQ, SEC_GK, SEC_GV, SEC_GZ, SEC_SQ, SEC_SK, SEC_SV, SEC_HQ, SEC_HF, SEC_HI, SEC_HG = range(11)
N_SEC = 11
_W_IN_AB = 4 * MIX_WIDTH
_W_IN_SB = _W_IN_AB + 2 * N_HEAD
_W_IN_GATE = _W_IN_SB + 7 * MIX_WIDTH


def _vmem_limit(*block_bytes):
    need = 2 * sum(block_bytes) + 2 * max(block_bytes)
    return int(min(max(need, 16 * 1024 * 1024), VMEM_LIMIT_CAP))


def _sigmoid(x):
    return 1.0 / (1.0 + jnp.exp(-x))


def _silu(x):
    return x * _sigmoid(x)


def _softplus(x):
    return jnp.maximum(x, 0.0) + jnp.log1p(jnp.exp(-jnp.abs(x)))


def _dot(a, b):
    return jnp.dot(a.astype(BF16), b.astype(BF16), preferred_element_type=F32)


def _dot_nt(a, b):
    return lax.dot_general(a.astype(BF16), b.astype(BF16), (((1,), (1,)), ((), ())),
                           preferred_element_type=F32)


def _dot_tn(a, b):
    return lax.dot_general(a.astype(BF16), b.astype(BF16), (((0,), (0,)), ((), ())),
                           preferred_element_type=F32)


def _split(x):
    hi = x.astype(BF16)
    lo = (x - hi.astype(F32)).astype(BF16)
    return hi, lo


def _dot_sel_lhs(sel, x):
    hi, lo = _split(x)
    d = functools.partial(jnp.dot, preferred_element_type=F32)
    return d(sel, hi) + d(sel, lo)


def _dot_sel_rhs(x, sel):
    hi, lo = _split(x)
    d = functools.partial(jnp.dot, preferred_element_type=F32)
    return d(hi, sel) + d(lo, sel)


def _run_interleaved(gens):
    results = [None] * len(gens)
    live = list(range(len(gens)))
    while live:
        for i in list(live):
            try:
                next(gens[i])
            except StopIteration as done:
                results[i] = done.value
                live.remove(i)
    return results


def _row_tile(m, cap):
    return max(t for t in range(16, cap + 1, 16) if m % t == 0)


def _rmsnorm_kernel(x_ref, w_ref, o_ref):
    x = x_ref[...]
    ms = jnp.mean(x * x, axis=-1, keepdims=True)
    o_ref[...] = (x * lax.rsqrt(ms + RMS_EPS) * w_ref[...]).astype(o_ref.dtype)


def _rmsnorm(x, w, out_dtype, *, tm=None, grid=None, in_map=None, out_map=None, out_rows=None):
    m, d = x.shape
    if tm is None:
        tm = _row_tile(m, 256)
    if grid is None:
        grid, in_map, out_map, out_rows = (m // tm,), (lambda i: (i, 0)), (lambda i: (i, 0)), m
    return pl.pallas_call(
        _rmsnorm_kernel,
        grid=grid,
        in_specs=[pl.BlockSpec((tm, d), in_map),
                  pl.BlockSpec((1, d), lambda *g: (0, 0))],
        out_specs=pl.BlockSpec((tm, d), out_map),
        out_shape=jax.ShapeDtypeStruct((out_rows, d), out_dtype),
        compiler_params=pltpu.CompilerParams(
            dimension_semantics=("arbitrary",) * len(grid),
            vmem_limit_bytes=_vmem_limit(tm * d * 4, tm * d * 4)),
        name="rmsnorm",
    )(x, w.reshape(1, d))


def _mm_kernel(a_ref, w_ref, o_ref):
    o_ref[...] = jnp.dot(a_ref[...], w_ref[...], preferred_element_type=F32).astype(o_ref.dtype)


def _mm_headmajor_kernel(a_ref, w_ref, o_ref):
    y = jnp.dot(a_ref[...], w_ref[...], preferred_element_type=F32)
    for j in range(o_ref.shape[0]):
        o_ref[j] = y[:, j * LANES:(j + 1) * LANES].astype(o_ref.dtype)


def _mm_sigmoid_kernel(a_ref, w_ref, o_ref):
    y = jnp.dot(a_ref[...], w_ref[...], preferred_element_type=F32)
    o_ref[...] = _sigmoid(y).astype(o_ref.dtype)


def _mm_residual_kernel(a_ref, w_ref, r_ref, o_ref):
    y = jnp.dot(a_ref[...], w_ref[...], preferred_element_type=F32)
    o_ref[...] = r_ref[...] + y


def _mm_swiglu_kernel(a_ref, wg_ref, wu_ref, o_ref):
    a = a_ref[...]
    g = jnp.dot(a, wg_ref[...], preferred_element_type=F32)
    u = jnp.dot(a, wu_ref[...], preferred_element_type=F32)
    o_ref[...] = (_silu(g) * u).astype(o_ref.dtype)


def _dense(a, w, *, tm, tn, mode="plain", out_dtype=F32, residual=None, w2=None, name="dense"):
    m, k = a.shape
    n = w.shape[1]
    grid = (m // tm, n // tn)
    a_spec = pl.BlockSpec((tm, k), lambda i, j: (i, 0))
    w_spec = pl.BlockSpec((k, tn), lambda i, j: (0, j))
    o_spec = pl.BlockSpec((tm, tn), lambda i, j: (i, j))
    out_shape = jax.ShapeDtypeStruct((m, n), out_dtype)
    out_bytes = tm * tn * jnp.dtype(out_dtype).itemsize
    blocks = [tm * k * 2, k * tn * 2, out_bytes, tm * tn * 4]
    if mode == "plain":
        kern, ins, specs = _mm_kernel, (a, w), [a_spec, w_spec]
    elif mode == "headmajor":
        kern, ins, specs = _mm_headmajor_kernel, (a, w), [a_spec, w_spec]
        o_spec = pl.BlockSpec((tn // LANES, tm, LANES), lambda i, j: (j, i, 0))
        out_shape = jax.ShapeDtypeStruct((n // LANES, m, LANES), out_dtype)
    elif mode == "sigmoid":
        kern, ins, specs = _mm_sigmoid_kernel, (a, w), [a_spec, w_spec]
    elif mode == "residual":
        kern, ins, specs = _mm_residual_kernel, (a, w, residual), [a_spec, w_spec, o_spec]
        blocks.append(tm * tn * 4)
    elif mode == "swiglu":
        kern, ins, specs = _mm_swiglu_kernel, (a, w, w2), [a_spec, w_spec, w_spec]
        blocks += [k * tn * 2, tm * tn * 4]
    else:
        raise ValueError(mode)
    return pl.pallas_call(
        kern, grid=grid, in_specs=specs, out_specs=o_spec, out_shape=out_shape,
        compiler_params=pltpu.CompilerParams(
            dimension_semantics=("arbitrary", "arbitrary"),
            vmem_limit_bytes=_vmem_limit(*blocks)),
        name=name,
    )(*ins)


def _mm_ktiled_kernel(a_ref, w_ref, r_ref, o_ref, acc_ref):
    kk = pl.program_id(2)
    y = jnp.dot(a_ref[...], w_ref[...], preferred_element_type=F32)

    @pl.when(kk == 0)
    def _():
        acc_ref[...] = r_ref[...] + y

    @pl.when(kk > 0)
    def _():
        acc_ref[...] += y

    @pl.when(kk == pl.num_programs(2) - 1)
    def _():
        o_ref[...] = acc_ref[...]


def _dense_ktiled_residual(a, w, residual, *, tm, tn, tk, name):
    m, k = a.shape
    n = w.shape[1]
    return pl.pallas_call(
        _mm_ktiled_kernel,
        grid=(m // tm, n // tn, k // tk),
        in_specs=[pl.BlockSpec((tm, tk), lambda i, j, kk: (i, kk)),
                  pl.BlockSpec((tk, tn), lambda i, j, kk: (kk, j)),
                  pl.BlockSpec((tm, tn), lambda i, j, kk: (i, j))],
        out_specs=pl.BlockSpec((tm, tn), lambda i, j, kk: (i, j)),
        out_shape=jax.ShapeDtypeStruct((m, n), F32),
        scratch_shapes=[pltpu.VMEM((tm, tn), F32)],
        compiler_params=pltpu.CompilerParams(
            dimension_semantics=("arbitrary", "arbitrary", "arbitrary"),
            vmem_limit_bytes=_vmem_limit(tm * tk * 2, tk * tn * 2, tm * tn * 4, tm * tn * 4,
                                         tm * tn * 4)),
        name=name,
    )(a, w, residual)


def _merge_kernel(o_ref, w_ref, g_ref, out_ref, acc_ref):
    br = pl.program_id(2)
    a = jnp.concatenate([o_ref[0, h] for h in range(N_HEAD)], axis=1)
    y = jnp.dot(a, w_ref[0], preferred_element_type=F32)
    contrib = g_ref[...].astype(F32) * y

    @pl.when(br == 0)
    def _():
        acc_ref[...] = contrib

    @pl.when(br > 0)
    def _():
        acc_ref[...] += contrib

    @pl.when(br == N_BRANCH - 1)
    def _():
        out_ref[...] = acc_ref[...].astype(out_ref.dtype)


def _merge(branches, w_br, gates, *, tm, tn):
    m = branches.shape[2]
    n_col = D_MODEL // tn
    return pl.pallas_call(
        _merge_kernel,
        grid=(m // tm, n_col, N_BRANCH),
        in_specs=[pl.BlockSpec((1, N_HEAD, tm, LANES), lambda i, j, g: (g, 0, i, 0)),
                  pl.BlockSpec((1, MIX_WIDTH, tn), lambda i, j, g: (g, 0, j)),
                  pl.BlockSpec((tm, tn), lambda i, j, g: (i, g * n_col + j))],
        out_specs=pl.BlockSpec((tm, tn), lambda i, j, g: (i, j)),
        out_shape=jax.ShapeDtypeStruct((m, D_MODEL), BF16),
        scratch_shapes=[pltpu.VMEM((tm, tn), F32)],
        compiler_params=pltpu.CompilerParams(
            dimension_semantics=("arbitrary", "arbitrary", "arbitrary"),
            vmem_limit_bytes=_vmem_limit(tm * MIX_WIDTH * 2, MIX_WIDTH * tn * 2, tm * tn * 2,
                                         tm * tn * 2, tm * tn * 4, tm * MIX_WIDTH * 2)),
        name="branch_merge",
    )(branches, w_br, gates)


def _lower_bound_kernel(p_ref, o_ref):
    p = p_ref[...]
    e = jnp.exp(p - jnp.max(p, axis=0, keepdims=True))
    s = e / jnp.sum(e, axis=0, keepdims=True)
    run = jnp.zeros_like(s[0:1])
    for l in range(DEPTH):
        run = run + s[l:l + 1]
        o_ref[l:l + 1, :] = run - s[0:1]


def _lower_bounds(p):
    return pl.pallas_call(
        _lower_bound_kernel,
        out_shape=jax.ShapeDtypeStruct(p.shape, F32),
        name="hgrn_lower_bounds",
    )(p)


def _gdn_kernel(q_ref, k_ref, v_ref, z_ref, ab_ref, hist_ref, s0_ref, cw_ref, hp_ref, nw_ref,
                o_ref, sout_ref, prev_ref, s_ref, *, chunk):
    c = pl.program_id(1)
    C = chunk

    @pl.when(c == 0)
    def _():
        prev_ref[...] = jnp.zeros_like(prev_ref)
        prev_ref[:, :, C - SUBLANES:, :] = hist_ref[0]
        s_ref[...] = s0_ref[0]

    row = lax.broadcasted_iota(jnp.int32, (C, LANES), 0)
    lane = lax.broadcasted_iota(jnp.int32, (C, LANES), 1)
    ii = lax.broadcasted_iota(jnp.int32, (C, C), 0)
    jj = lax.broadcasted_iota(jnp.int32, (C, C), 1)
    eye = ii == jj
    tri = jj <= ii
    strict = jj < ii
    same16 = (ii >> 4) == (jj >> 4)
    eye_f = jnp.where(eye, 1.0, 0.0).astype(F32)

    ab = ab_ref[...]
    g_all = -jnp.exp(hp_ref[0:1, :]) * _softplus(ab + hp_ref[1:2, :])
    beta_all = _sigmoid(ab)
    nw = nw_ref[...]

    def conv(x, xp, w):
        y = x * w[CONV_W - 1:CONV_W, :]
        for j in range(1, CONV_W):
            xs = jnp.where(row < j, pltpu.roll(xp, j, 0), pltpu.roll(x, j, 0))
            y = y + xs * w[CONV_W - 1 - j:CONV_W - j, :]
        return _silu(y)

    def l2n(x):
        return x * lax.rsqrt(jnp.sum(x * x, axis=-1, keepdims=True) + RMS_EPS)

    def load(h):
        raw = [r[h] for r in (q_ref, k_ref, v_ref)]
        prev = [prev_ref[p, h] for p in range(3)]
        taps = [cw_ref[p, h] for p in range(3)]
        return raw, prev, taps, z_ref[h], s_ref[h]

    def store(h, raw, o, s_new):
        for p in range(3):
            prev_ref[p, h] = raw[p]
        s_ref[h] = s_new
        o_ref[h] = o

    def compute(h, raw, prev, taps, z, s):
        q = l2n(conv(raw[0], prev[0], taps[0])) * (HEAD_DIM ** -0.5)
        k = l2n(conv(raw[1], prev[1], taps[1]))
        v = conv(raw[2], prev[2], taps[2])
        g_col = jnp.sum(jnp.where(lane == h, g_all, 0.0), axis=1, keepdims=True)
        b_col = jnp.sum(jnp.where(lane == h + N_HEAD, beta_all, 0.0), axis=1, keepdims=True)

        g_row = jnp.sum(jnp.where(eye, g_col, 0.0), axis=0, keepdims=True)
        gc_col = jnp.sum(jnp.where(tri, g_row, 0.0), axis=1, keepdims=True)
        gc_row = jnp.sum(jnp.where(eye, gc_col, 0.0), axis=0, keepdims=True)
        diff = gc_col - gc_row
        decay = jnp.where(tri, jnp.exp(jnp.where(tri, diff, 0.0)), 0.0)
        egc = jnp.exp(gc_col)
        g_last = gc_col[C - 1:C, :]

        kb = k * b_col
        rhs = jnp.concatenate([v * b_col, kb * egc], axis=1)
        k_dec = k * jnp.exp(g_last - gc_col)
        yield
        qk_kbk = _dot_nt(jnp.concatenate([q, kb], axis=0), k)
        attn = jnp.where(tri, qk_kbk[:C] * decay, 0.0)
        lower = jnp.where(strict, qk_kbk[C:] * decay, 0.0)

        ld = jnp.where(same16, lower, 0.0)
        lo = lower - ld
        yield
        p2 = _dot(ld, ld)
        yield
        p4 = _dot(p2, p2)
        a1 = _dot(eye_f - ld, eye_f + p2)
        yield
        p8 = _dot(p4, p4)
        a2 = _dot(a1, eye_f + p4)
        yield
        td = _dot(a2, eye_f + p8)
        yield
        y_nm = _dot(td, jnp.concatenate([rhs, lo], axis=1))
        y = y_nm[:, :2 * HEAD_DIM]
        nm = y_nm[:, 2 * HEAD_DIM:]
        yield
        n2 = _dot(nm, nm)
        yield
        zz = y + _dot(n2, y)
        yield
        sol = zz - _dot(nm, zz)
        u = sol[:, :HEAD_DIM]
        w = sol[:, HEAD_DIM:]
        yield
        ws_qs = _dot(jnp.concatenate([w, q * egc], axis=0), s)
        v_new = u - ws_qs[:C]
        yield
        o = ws_qs[C:] + _dot(attn, v_new)
        s_new = s * jnp.exp(g_last) + _dot_tn(k_dec, v_new)

        o = o * lax.rsqrt(jnp.mean(o * o, axis=-1, keepdims=True) + RMS_EPS) * nw
        return (o * _silu(z)).astype(o_ref.dtype), s_new

    def group(i, carry):
        heads = [i * GDN_HEADS_PER_STEP + u for u in range(GDN_HEADS_PER_STEP)]
        loaded = [load(h) for h in heads]
        results = _run_interleaved([compute(h, *vals) for h, vals in zip(heads, loaded)])
        for h, vals, (o, s_new) in zip(heads, loaded, results):
            store(h, vals[0], o, s_new)
        return carry

    lax.fori_loop(0, N_HEAD // GDN_HEADS_PER_STEP, group, 0)

    @pl.when(c == pl.num_programs(1) - 1)
    def _():
        sout_ref[0] = s_ref[...]


def _gdn(proj, ab, hist, s0, cw, hp, nw, *, batch, row0, t_pad, chunk):
    nc = t_pad // chunk
    rb0 = row0 // chunk

    def sec(s):
        return pl.BlockSpec((N_HEAD, chunk, LANES), lambda b, c: (s, rb0 + b * nc + c, 0))

    kern = functools.partial(_gdn_kernel, chunk=chunk)
    return pl.pallas_call(
        kern,
        grid=(batch, nc),
        in_specs=[sec(SEC_GQ), sec(SEC_GK), sec(SEC_GV), sec(SEC_GZ),
                  pl.BlockSpec((chunk, LANES), lambda b, c: (rb0 + b * nc + c, 0)),
                  pl.BlockSpec((1, 3, N_HEAD, SUBLANES, LANES), lambda b, c: (b, 0, 0, 0, 0)),
                  pl.BlockSpec((1, N_HEAD, HEAD_DIM, HEAD_DIM), lambda b, c: (b, 0, 0, 0)),
                  pl.BlockSpec((3, N_HEAD, SUBLANES, LANES), lambda b, c: (0, 0, 0, 0)),
                  pl.BlockSpec((SUBLANES, LANES), lambda b, c: (0, 0)),
                  pl.BlockSpec((1, LANES), lambda b, c: (0, 0))],
        out_specs=[pl.BlockSpec((N_HEAD, chunk, LANES), lambda b, c: (0, b * nc + c, 0)),
                   pl.BlockSpec((1, N_HEAD, HEAD_DIM, HEAD_DIM), lambda b, c: (b, 0, 0, 0))],
        out_shape=[jax.ShapeDtypeStruct((N_HEAD, batch * t_pad, LANES), BF16),
                   jax.ShapeDtypeStruct((batch, N_HEAD, HEAD_DIM, HEAD_DIM), F32)],
        scratch_shapes=[pltpu.VMEM((3, N_HEAD, chunk, LANES), F32),
                        pltpu.VMEM((N_HEAD, HEAD_DIM, HEAD_DIM), F32)],
        compiler_params=pltpu.CompilerParams(
            dimension_semantics=("arbitrary", "arbitrary"),
            vmem_limit_bytes=32 * 1024 * 1024),
        name="gdn_mixer",
    )(proj, proj, proj, proj, ab, hist, s0, cw, hp, nw)


def _hgrn_kernel(q_ref, f_ref, i_ref, g_ref, lb_ref, s0_ref, nw_ref, o_ref, sout_ref, s_ref,
                 *, chunk):
    c = pl.program_id(1)
    C = chunk
    halves = [r for r in (32, 16, 8) if 2 * r <= C]

    @pl.when(c == 0)
    def _():
        s_ref[...] = s0_ref[0]

    ii = lax.broadcasted_iota(jnp.int32, (C, C), 0)
    jj = lax.broadcasted_iota(jnp.int32, (C, C), 1)
    sel_blocks = [jnp.where(jj <= ii, 1.0, 0.0)]
    level_masks = []
    for r in halves:
        ref_row = (ii & ~(2 * r - 1)) + (r - 1)
        sel_blocks.append(jnp.where(jj <= ref_row, 1.0, 0.0))
        same_pair = (ii & ~(2 * r - 1)) == (jj & ~(2 * r - 1))
        level_masks.append(same_pair & ((ii & r) != 0) & ((jj & r) == 0))
    sel = jnp.concatenate(sel_blocks, axis=0).astype(BF16)
    nw = nw_ref[...]
    row8 = lax.broadcasted_iota(jnp.int32, (SUBLANES, LANES), 0)

    def load(h):
        return q_ref[h], f_ref[h], i_ref[h], g_ref[h], lb_ref[h], s_ref[h]

    def compute(q_raw, f_raw, v, g_raw, lb, st):
        q = _silu(q_raw)
        f = lb + (1.0 - lb) * _sigmoid(f_raw)
        logf = jnp.log(f)
        k = 1.0 - f
        yield
        cums = _dot_sel_lhs(sel, logf)
        b = cums[:C]
        q_dec = q * jnp.exp(b)
        b_last = b[C - 1:C]
        k_dec = k * jnp.exp(b_last - b)
        yield
        attn = jnp.zeros((C, C), F32)
        for l, mask in enumerate(level_masks):
            bref = cums[(l + 1) * C:(l + 2) * C]
            qt = q * jnp.exp(jnp.minimum(b - bref, 0.0))
            kt = k * jnp.exp(jnp.minimum(bref - b, 0.0))
            attn = attn + jnp.where(mask, _dot_nt(qt, kt), 0.0)
        o_inter = _dot_nt(q_dec, st)
        st_new = st * jnp.exp(b_last) + _dot_tn(v, k_dec)
        yield
        o = o_inter + _dot(attn, v)

        diag = []
        for d in range(C // SUBLANES):
            sl = slice(d * SUBLANES, (d + 1) * SUBLANES)
            qd, kd, bd, vd = q[sl], k[sl], b[sl], v[sl]
            od = jnp.zeros((SUBLANES, LANES), F32)
            for s in range(SUBLANES):
                e = jnp.exp(jnp.minimum(bd - bd[s:s + 1], 0.0))
                wgt = jnp.sum(qd * kd[s:s + 1] * e, axis=1, keepdims=True)
                od = od + jnp.where(row8 >= s, wgt, 0.0) * vd[s:s + 1]
            diag.append(od)
        o = o + jnp.concatenate(diag, axis=0)

        o = o * lax.rsqrt(jnp.mean(o * o, axis=-1, keepdims=True) + RMS_EPS) * nw
        return (o * _silu(g_raw)).astype(o_ref.dtype), st_new

    def group(i, carry):
        heads = [i * HGRN_HEADS_PER_STEP + u for u in range(HGRN_HEADS_PER_STEP)]
        results = _run_interleaved([compute(*vals) for vals in [load(h) for h in heads]])
        for h, (o, st_new) in zip(heads, results):
            s_ref[h] = st_new
            o_ref[h] = o
        return carry

    lax.fori_loop(0, N_HEAD // HGRN_HEADS_PER_STEP, group, 0)

    @pl.when(c == pl.num_programs(1) - 1)
    def _():
        sout_ref[0] = s_ref[...]


def _hgrn(proj, lb, s0_t, nw, *, batch, row0, t_pad, chunk):
    nc = t_pad // chunk
    rb0 = row0 // chunk

    def sec(s):
        return pl.BlockSpec((N_HEAD, chunk, LANES), lambda b, c: (s, rb0 + b * nc + c, 0))

    kern = functools.partial(_hgrn_kernel, chunk=chunk)
    return pl.pallas_call(
        kern,
        grid=(batch, nc),
        in_specs=[sec(SEC_HQ), sec(SEC_HF), sec(SEC_HI), sec(SEC_HG),
                  pl.BlockSpec((N_HEAD, 1, LANES), lambda b, c: (0, 0, 0)),
                  pl.BlockSpec((1, N_HEAD, HEAD_DIM, HEAD_DIM), lambda b, c: (b, 0, 0, 0)),
                  pl.BlockSpec((1, LANES), lambda b, c: (0, 0))],
        out_specs=[pl.BlockSpec((N_HEAD, chunk, LANES), lambda b, c: (0, b * nc + c, 0)),
                   pl.BlockSpec((1, N_HEAD, HEAD_DIM, HEAD_DIM), lambda b, c: (b, 0, 0, 0))],
        out_shape=[jax.ShapeDtypeStruct((N_HEAD, batch * t_pad, LANES), BF16),
                   jax.ShapeDtypeStruct((batch, N_HEAD, HEAD_DIM, HEAD_DIM), F32)],
        scratch_shapes=[pltpu.VMEM((N_HEAD, HEAD_DIM, HEAD_DIM), F32)],
        compiler_params=pltpu.CompilerParams(
            dimension_semantics=("arbitrary", "arbitrary"),
            vmem_limit_bytes=32 * 1024 * 1024),
        name="hgrn_mixer",
    )(proj, proj, proj, proj, lb, s0_t, nw)


def _sb_kernel(*refs, q_block, t_new, t_past, pad_front):
    if t_past:
        q_ref, kn_ref, vn_ref, kp_ref, vp_ref, o_ref, ko_ref, vo_ref, kb_ref, vb_ref = refs
    else:
        q_ref, kn_ref, vn_ref, o_ref, ko_ref, vo_ref, kb_ref, vb_ref = refs
    qi = pl.program_id(2)
    t_all = t_past + t_new
    t_scr = kb_ref.shape[0]
    n_sub = SB_KEY_SLAB // LANES

    @pl.when(qi == 0)
    def _():
        k_new = kn_ref[0]
        v_new = vn_ref[0]
        ko_ref[0] = k_new[pad_front:, :]
        vo_ref[0] = v_new[pad_front:, :]
        if t_past:
            kb_ref[0:t_past, :] = kp_ref[0].astype(BF16)
            vb_ref[0:t_past, :] = vp_ref[0].astype(BF16)
        kb_ref[t_past:t_all, :] = k_new.astype(BF16)
        vb_ref[t_past:t_all, :] = v_new.astype(BF16)
        if t_scr > t_all:
            kb_ref[t_all:, :] = jnp.zeros((t_scr - t_all, LANES), BF16)
            vb_ref[t_all:, :] = jnp.zeros((t_scr - t_all, LANES), BF16)

    q = q_ref[0].astype(BF16)
    q_pos = t_past + qi * q_block + lax.broadcasted_iota(jnp.int32, (q_block, SB_KEY_SLAB), 0)
    lane = lax.broadcasted_iota(jnp.int32, (q_block, SB_KEY_SLAB), 1)
    jr = lax.broadcasted_iota(jnp.int32, (LANES, 2 * LANES), 0)
    sc = lax.broadcasted_iota(jnp.int32, (LANES, 2 * LANES), 1)
    suffix_and_total = jnp.where((sc >= LANES) | (jr > sc), 1.0, 0.0).astype(BF16)
    slab_hi = (t_past + qi * q_block + q_block - 1) // SB_KEY_SLAB
    scale = HEAD_DIM ** -0.5

    def cond(carry):
        it, _, _, active = carry
        return jnp.logical_and(it <= slab_hi, active > 0)

    def body(carry):
        it, run, acc, _ = carry
        slab = slab_hi - it
        start = pl.multiple_of(slab * SB_KEY_SLAB, SB_KEY_SLAB)
        kt = kb_ref[pl.ds(start, SB_KEY_SLAB), :]
        vt = vb_ref[pl.ds(start, SB_KEY_SLAB), :]
        z = lax.dot_general(q, kt, (((1,), (1,)), ((), ())), preferred_element_type=F32) * scale
        k_pos = slab * SB_KEY_SLAB + lane
        vis = (k_pos < q_pos) & (k_pos >= pad_front)
        sp = _softplus(z)
        log_keep = jnp.where(vis, -sp, 0.0)
        between = [None] * n_sub
        for j in reversed(range(n_sub)):
            sums = _dot_sel_rhs(log_keep[:, j * LANES:(j + 1) * LANES], suffix_and_total)
            between[j] = sums[:, :LANES] + run
            run = run + sums[:, LANES:]
        a = jnp.where(vis, jnp.exp((z - sp) + jnp.concatenate(between, axis=1)), 0.0)
        acc = acc + jnp.dot(a.astype(BF16), vt, preferred_element_type=F32)
        return it + 1, run, acc, (jnp.max(run) > EXP_UNDERFLOW).astype(jnp.int32)

    zero = jnp.zeros((q_block, LANES), F32)
    _, _, acc, _ = lax.while_loop(cond, body, (jnp.int32(0), zero, zero, jnp.int32(1)))
    o_ref[0] = acc.astype(o_ref.dtype)


def _sb(proj, past_k, past_v, *, batch, row0, t_pad, q_block, t_past, pad_front):
    nq = t_pad // q_block
    rbq0 = row0 // q_block
    rbk0 = row0 // t_pad
    t_scr = -(-(t_past + t_pad) // SB_KEY_SLAB) * SB_KEY_SLAB
    t_out = t_pad - pad_front
    sec_q = SEC_SQ * N_HEAD
    sec_k = SEC_SK * N_HEAD
    sec_v = SEC_SV * N_HEAD
    in_specs = [pl.BlockSpec((1, q_block, LANES), lambda b, h, i: (sec_q + h, rbq0 + b * nq + i, 0)),
                pl.BlockSpec((1, t_pad, LANES), lambda b, h, i: (sec_k + h, rbk0 + b, 0)),
                pl.BlockSpec((1, t_pad, LANES), lambda b, h, i: (sec_v + h, rbk0 + b, 0))]
    args = [proj, proj, proj]
    if t_past:
        in_specs += [pl.BlockSpec((1, t_past, LANES), lambda b, h, i: (b, 0, h)),
                     pl.BlockSpec((1, t_past, LANES), lambda b, h, i: (b, 0, h))]
        args += [past_k, past_v]
    kv_out = pl.BlockSpec((1, t_out, LANES), lambda b, h, i: (b, 0, h))
    kern = functools.partial(_sb_kernel, q_block=q_block, t_new=t_pad, t_past=t_past,
                             pad_front=pad_front)
    return pl.pallas_call(
        kern,
        grid=(batch, N_HEAD, nq),
        in_specs=in_specs,
        out_specs=[pl.BlockSpec((1, q_block, LANES), lambda b, h, i: (h, b * nq + i, 0)),
                   kv_out, kv_out],
        out_shape=[jax.ShapeDtypeStruct((N_HEAD, batch * t_pad, LANES), BF16),
                   jax.ShapeDtypeStruct((batch, t_out, MIX_WIDTH), F32),
                   jax.ShapeDtypeStruct((batch, t_out, MIX_WIDTH), F32)],
        scratch_shapes=[pltpu.VMEM((t_scr, LANES), BF16), pltpu.VMEM((t_scr, LANES), BF16)],
        compiler_params=pltpu.CompilerParams(
            dimension_semantics=("arbitrary", "arbitrary", "arbitrary"),
            vmem_limit_bytes=32 * 1024 * 1024),
        name="stickbreak_attention",
    )(*args)


def kernel(x_prompt, x_sample, cache_sb_k, cache_sb_v, state_gdn_conv, state_gdn, state_hgrn,
           meta_tokens, norm1_w, w_in, gdn_conv_w, gdn_a_log, gdn_dt_bias, gdn_norm_w,
           hgrn_lb_param, hgrn_norm_w, w_branch, w_o, norm2_w, w_ff_gate, w_ff_up, w_ff_down,
           final_norm_w):
    b_p, seq, _ = x_prompt.shape
    b_s, t_s, _ = x_sample.shape
    t_p = N_META + seq
    sb_qb = SB_Q_BLOCK_P if t_p > SB_Q_BLOCK_P else CHUNK_P
    t_pp = -(-t_p // sb_qb) * sb_qb
    pad_front = t_pp - t_p
    m_p = b_p * t_pp
    m_s = b_s * t_s
    m_all = m_p + m_s
    past_len = cache_sb_k.shape[2]
    tm = _row_tile(m_all, 1088)
    t_skip = pad_front + N_META
    assert t_pp % CHUNK_P == 0 and pad_front % SUBLANES == 0 and t_skip % CHUNK_P == 0

    meta = jnp.broadcast_to(meta_tokens[None], (b_p, N_META, D_MODEL)).astype(F32)
    xp = jnp.concatenate([jnp.zeros((b_p, pad_front, D_MODEL), F32), meta, x_prompt], axis=1)
    x = jnp.concatenate([xp.reshape(m_p, D_MODEL), x_sample.reshape(m_s, D_MODEL)], axis=0)

    lower_bounds = _lower_bounds(hgrn_lb_param)

    zero_hist = jnp.zeros((b_p, 3, N_HEAD, SUBLANES, LANES), F32)
    zero_state = jnp.zeros((b_p, N_HEAD, HEAD_DIM, HEAD_DIM), F32)
    past_k2 = cache_sb_k.reshape(DEPTH, b_s, past_len, MIX_WIDTH)
    past_v2 = cache_sb_v.reshape(DEPTH, b_s, past_len, MIX_WIDTH)

    outs = {name: [] for name in ("kp", "vp", "ks", "vs", "cp", "cs", "gp", "gs", "hp", "hs")}
    for l in range(DEPTH):
        wl = w_in[l]
        w_mix = jnp.concatenate([wl[:, :_W_IN_AB], wl[:, _W_IN_SB:_W_IN_GATE]], axis=1).astype(BF16)
        w_ab = jnp.pad(wl[:, _W_IN_AB:_W_IN_SB], ((0, 0), (0, LANES - 2 * N_HEAD))).astype(BF16)
        w_gate_in = wl[:, _W_IN_GATE:].astype(BF16)

        h = _rmsnorm(x, norm1_w[l], BF16)
        proj = _dense(h, w_mix, tm=tm, tn=1024, mode="headmajor", name="proj_mixers")
        ab = _dense(h, w_ab, tm=tm, tn=LANES, name="proj_ab")
        gates = _dense(h, w_gate_in, tm=tm, tn=1024, mode="sigmoid", out_dtype=BF16, name="proj_gates")

        cw = gdn_conv_w[l].reshape(CONV_W, 3, N_HEAD, HEAD_DIM).transpose(1, 2, 0, 3)
        cw = jnp.pad(cw, ((0, 0), (0, 0), (0, SUBLANES - CONV_W), (0, 0)))
        hp = jnp.zeros((SUBLANES, LANES), F32)
        hp = hp.at[0, :N_HEAD].set(gdn_a_log[l]).at[1, :N_HEAD].set(gdn_dt_bias[l])
        gnw = gdn_norm_w[l].reshape(1, HEAD_DIM)
        hist_s = state_gdn_conv[l].reshape(b_s, CONV_W - 1, 3, N_HEAD, HEAD_DIM).transpose(0, 2, 3, 1, 4)
        hist_s = jnp.pad(hist_s, ((0, 0), (0, 0), (0, 0), (SUBLANES - (CONV_W - 1), 0), (0, 0)))
        oa_p, gs_p = _gdn(proj, ab, zero_hist, zero_state, cw, hp, gnw,
                          batch=b_p, row0=0, t_pad=t_pp, chunk=CHUNK_P)
        oa_s, gs_s = _gdn(proj, ab, hist_s, state_gdn[l], cw, hp, gnw,
                          batch=b_s, row0=m_p, t_pad=t_s, chunk=t_s)

        ob_p, k_p, v_p = _sb(proj, None, None, batch=b_p, row0=0, t_pad=t_pp, q_block=sb_qb,
                             t_past=0, pad_front=pad_front)
        ob_s, k_s, v_s = _sb(proj, past_k2[l], past_v2[l], batch=b_s, row0=m_p, t_pad=t_s,
                             q_block=t_s, t_past=past_len, pad_front=0)

        lb = lower_bounds[l].reshape(N_HEAD, 1, HEAD_DIM)
        hnw = hgrn_norm_w[l].reshape(1, HEAD_DIM)
        oc_p, hs_p = _hgrn(proj, lb, zero_state, hnw, batch=b_p, row0=0, t_pad=t_pp, chunk=CHUNK_P)
        oc_s, hs_s = _hgrn(proj, lb, state_hgrn[l].swapaxes(-1, -2), hnw, batch=b_s, row0=m_p,
                           t_pad=t_s, chunk=t_s)

        branches = jnp.stack([jnp.concatenate([oa_p, oa_s], axis=1),
                              jnp.concatenate([ob_p, ob_s], axis=1),
                              jnp.concatenate([oc_p, oc_s], axis=1)])
        merged = _merge(branches, w_branch[l].astype(BF16), gates, tm=tm, tn=1024)
        x = _dense(merged, w_o[l].astype(BF16), tm=tm, tn=1024, mode="residual", residual=x,
                   name="out_proj")

        h2 = _rmsnorm(x, norm2_w[l], BF16)
        pad_ff = ((0, 0), (0, D_FF_PAD - D_FF))
        hmid = _dense(h2, jnp.pad(w_ff_gate[l], pad_ff).astype(BF16), tm=tm, tn=512, mode="swiglu",
                      out_dtype=BF16, w2=jnp.pad(w_ff_up[l], pad_ff).astype(BF16), name="ffn_gate_up")
        w_down = jnp.pad(w_ff_down[l], ((0, D_FF_PAD - D_FF), (0, 0))).astype(BF16)
        x = _dense_ktiled_residual(hmid, w_down, x, tm=tm, tn=1024, tk=1024, name="ffn_down")

        outs["kp"].append(k_p.reshape(b_p, t_p, N_HEAD, HEAD_DIM))
        outs["vp"].append(v_p.reshape(b_p, t_p, N_HEAD, HEAD_DIM))
        outs["ks"].append(k_s.reshape(b_s, t_s, N_HEAD, HEAD_DIM))
        outs["vs"].append(v_s.reshape(b_s, t_s, N_HEAD, HEAD_DIM))
        n_qkv = 3 * N_HEAD
        qkv_p = proj[:n_qkv, :m_p].reshape(n_qkv, b_p, t_pp, HEAD_DIM)[:, :, t_pp - (CONV_W - 1):]
        outs["cp"].append(qkv_p.transpose(1, 2, 0, 3).reshape(b_p, CONV_W - 1, 3 * MIX_WIDTH))
        qkv_s = proj[:n_qkv, m_p:].reshape(n_qkv, b_s, t_s, HEAD_DIM)[:, :, t_s - (CONV_W - 1):]
        outs["cs"].append(qkv_s.transpose(1, 2, 0, 3).reshape(b_s, CONV_W - 1, 3 * MIX_WIDTH))
        outs["gp"].append(gs_p)
        outs["gs"].append(gs_s)
        outs["hp"].append(hs_p.swapaxes(-1, -2))
        outs["hs"].append(hs_s.swapaxes(-1, -2))

    tf = CHUNK_P
    y_prompt = _rmsnorm(
        x, final_norm_w, F32, tm=tf, grid=(b_p, seq // tf),
        in_map=lambda b, i: (b * (t_pp // tf) + t_skip // tf + i, 0),
        out_map=lambda b, i: (b * (seq // tf) + i, 0), out_rows=b_p * seq).reshape(b_p, seq, D_MODEL)
    ts = _row_tile(m_s, 256)
    y_sample = _rmsnorm(
        x, final_norm_w, F32, tm=ts, grid=(m_s // ts,),
        in_map=lambda i: (m_p // ts + i, 0), out_map=lambda i: (i, 0),
        out_rows=m_s).reshape(b_s, t_s, D_MODEL)
    st = {k: jnp.stack(v) for k, v in outs.items()}
    return (y_prompt, y_sample, st["kp"], st["vp"], st["ks"], st["vs"], st["cp"], st["cs"],
            st["gp"], st["gs"], st["hp"], st["hs"])
```

```python
import functools

import jax
import jax.numpy as jnp
from jax import lax
from jax.experimental import pallas as pl
from jax.experimental.pallas import tpu as pltpu

F32 = jnp.float32
BF16 = jnp.bfloat16

D_MODEL = 4096
DEPTH = 4
N_HEAD = 16
HEAD_DIM = 128
MIX_WIDTH = N_HEAD * HEAD_DIM
N_META = 16
CONV_W = 4
D_FF = 11008
RMS_EPS = 1e-6
N_BRANCH = 3

LANES = 128
SUBLANES = 8
VMEM_LIMIT_CAP = 60 * 1024 * 1024

CHUNK_P = 64
SB_Q_BLOCK_P = 192
SB_KEY_SLAB = 3 * LANES
D_FF_PAD = -(-D_FF // 1024) * 1024
GDN_HEADS_PER_STEP = 16
HGRN_HEADS_PER_STEP = 4
EXP_UNDERFLOW = -104.0
SEC_GQ, SEC_GK, SEC_GV, SEC_GZ = range(4)
SEC_SQ, SEC_SK, SEC_SV, SEC_HQ, SEC_HF, SEC_HI, SEC_HG = range(7)
BR_GDN, BR_SB, BR_HGRN = range(3)
_W_IN_AB = 4 * MIX_WIDTH
_W_IN_SB = _W_IN_AB + 2 * N_HEAD
_W_IN_GATE = _W_IN_SB + 7 * MIX_WIDTH


def _vmem_limit(*block_bytes):
    need = 2 * sum(block_bytes) + 2 * max(block_bytes)
    return int(min(max(need, 16 * 1024 * 1024), VMEM_LIMIT_CAP))


def _sigmoid(x):
    return 1.0 / (1.0 + jnp.exp(-x))


def _silu(x):
    return x * _sigmoid(x)


def _softplus(x):
    return jnp.maximum(x, 0.0) + jnp.log1p(jnp.exp(-jnp.abs(x)))


def _dot(a, b):
    return jnp.dot(a.astype(BF16), b.astype(BF16), preferred_element_type=F32)


def _dot_nt(a, b):
    return lax.dot_general(a.astype(BF16), b.astype(BF16), (((1,), (1,)), ((), ())),
                           preferred_element_type=F32)


def _dot_tn(a, b):
    return lax.dot_general(a.astype(BF16), b.astype(BF16), (((0,), (0,)), ((), ())),
                           preferred_element_type=F32)


def _split(x):
    hi = x.astype(BF16)
    lo = (x - hi.astype(F32)).astype(BF16)
    return hi, lo


def _dot_sel_lhs(sel, x):
    hi, lo = _split(x)
    d = functools.partial(jnp.dot, preferred_element_type=F32)
    return d(sel, hi) + d(sel, lo)


def _dot_sel_rhs(x, sel):
    hi, lo = _split(x)
    d = functools.partial(jnp.dot, preferred_element_type=F32)
    return d(hi, sel) + d(lo, sel)


def _run_interleaved(gens):
    results = [None] * len(gens)
    live = list(range(len(gens)))
    while live:
        for i in list(live):
            try:
                next(gens[i])
            except StopIteration as done:
                results[i] = done.value
                live.remove(i)
    return results


def _row_tile(m, cap):
    return max(t for t in range(16, cap + 1, 16) if m % t == 0)


def _rmsnorm_kernel(x_ref, w_ref, o_ref):
    x = x_ref[...]
    ms = jnp.mean(x * x, axis=-1, keepdims=True)
    o_ref[...] = (x * lax.rsqrt(ms + RMS_EPS) * w_ref[...]).astype(o_ref.dtype)


def _rmsnorm(x, w, out_dtype, *, tm=None, grid=None, in_map=None, out_map=None, out_rows=None):
    m, d = x.shape
    if tm is None:
        tm = _row_tile(m, 256)
    if grid is None:
        grid, in_map, out_map, out_rows = (m // tm,), (lambda i: (i, 0)), (lambda i: (i, 0)), m
    return pl.pallas_call(
        _rmsnorm_kernel,
        grid=grid,
        in_specs=[pl.BlockSpec((tm, d), in_map),
                  pl.BlockSpec((1, d), lambda *g: (0, 0))],
        out_specs=pl.BlockSpec((tm, d), out_map),
        out_shape=jax.ShapeDtypeStruct((out_rows, d), out_dtype),
        compiler_params=pltpu.CompilerParams(
            dimension_semantics=("arbitrary",) * len(grid),
            vmem_limit_bytes=_vmem_limit(tm * d * 4, tm * d * 4)),
        name="rmsnorm",
    )(x, w.reshape(1, d))


def _mm_kernel(a_ref, w_ref, o_ref):
    o_ref[...] = jnp.dot(a_ref[...], w_ref[...], preferred_element_type=F32).astype(o_ref.dtype)


def _mm_headmajor_kernel(a_ref, w_ref, o_ref):
    y = jnp.dot(a_ref[...], w_ref[...], preferred_element_type=F32)
    for j in range(o_ref.shape[0]):
        o_ref[j] = y[:, j * LANES:(j + 1) * LANES].astype(o_ref.dtype)


def _mm_sigmoid_kernel(a_ref, w_ref, o_ref):
    y = jnp.dot(a_ref[...], w_ref[...], preferred_element_type=F32)
    o_ref[...] = _sigmoid(y).astype(o_ref.dtype)


def _mm_residual_kernel(a_ref, w_ref, r_ref, o_ref):
    y = jnp.dot(a_ref[...], w_ref[...], preferred_element_type=F32)
    o_ref[...] = r_ref[...] + y


def _mm_swiglu_kernel(a_ref, wg_ref, wu_ref, o_ref):
    a = a_ref[...]
    g = jnp.dot(a, wg_ref[...], preferred_element_type=F32)
    u = jnp.dot(a, wu_ref[...], preferred_element_type=F32)
    o_ref[...] = (_silu(g) * u).astype(o_ref.dtype)


def _dense(a, w, layer, *, tm, tn, mode="plain", out_dtype=F32, residual=None, w2=None, name="dense"):
    m, k = a.shape
    n = w.shape[2]
    grid = (m // tm, n // tn)
    a_spec = pl.BlockSpec((tm, k), lambda i, j: (i, 0))
    w_spec = pl.BlockSpec((None, k, tn), lambda i, j: (layer, 0, j))
    o_spec = pl.BlockSpec((tm, tn), lambda i, j: (i, j))
    out_shape = jax.ShapeDtypeStruct((m, n), out_dtype)
    out_bytes = tm * tn * jnp.dtype(out_dtype).itemsize
    blocks = [tm * k * 2, k * tn * 2, out_bytes, tm * tn * 4]
    if mode == "plain":
        kern, ins, specs = _mm_kernel, (a, w), [a_spec, w_spec]
    elif mode == "headmajor":
        kern, ins, specs = _mm_headmajor_kernel, (a, w), [a_spec, w_spec]
        o_spec = pl.BlockSpec((tn // LANES, tm, LANES), lambda i, j: (j, i, 0))
        out_shape = jax.ShapeDtypeStruct((n // LANES, m, LANES), out_dtype)
    elif mode == "sigmoid":
        kern, ins, specs = _mm_sigmoid_kernel, (a, w), [a_spec, w_spec]
    elif mode == "residual":
        kern, ins, specs = _mm_residual_kernel, (a, w, residual), [a_spec, w_spec, o_spec]
        blocks.append(tm * tn * 4)
    elif mode == "swiglu":
        kern, ins, specs = _mm_swiglu_kernel, (a, w, w2), [a_spec, w_spec, w_spec]
        blocks += [k * tn * 2, tm * tn * 4]
    else:
        raise ValueError(mode)
    return pl.pallas_call(
        kern, grid=grid, in_specs=specs, out_specs=o_spec, out_shape=out_shape,
        compiler_params=pltpu.CompilerParams(
            dimension_semantics=("arbitrary", "arbitrary"),
            vmem_limit_bytes=_vmem_limit(*blocks)),
        name=name,
    )(*ins)


def _mm_ktiled_kernel(a_ref, w_ref, r_ref, o_ref, acc_ref):
    kk = pl.program_id(2)
    y = jnp.dot(a_ref[...], w_ref[...], preferred_element_type=F32)

    @pl.when(kk == 0)
    def _():
        acc_ref[...] = r_ref[...] + y

    @pl.when(kk > 0)
    def _():
        acc_ref[...] += y

    @pl.when(kk == pl.num_programs(2) - 1)
    def _():
        o_ref[...] = acc_ref[...]


def _dense_ktiled_residual(a, w, layer, residual, *, tm, tn, tk, name):
    m, k = a.shape
    n = w.shape[2]
    return pl.pallas_call(
        _mm_ktiled_kernel,
        grid=(m // tm, n // tn, k // tk),
        in_specs=[pl.BlockSpec((tm, tk), lambda i, j, kk: (i, kk)),
                  pl.BlockSpec((None, tk, tn), lambda i, j, kk: (layer, kk, j)),
                  pl.BlockSpec((tm, tn), lambda i, j, kk: (i, j))],
        out_specs=pl.BlockSpec((tm, tn), lambda i, j, kk: (i, j)),
        out_shape=jax.ShapeDtypeStruct((m, n), F32),
        scratch_shapes=[pltpu.VMEM((tm, tn), F32)],
        compiler_params=pltpu.CompilerParams(
            dimension_semantics=("arbitrary", "arbitrary", "arbitrary"),
            vmem_limit_bytes=_vmem_limit(tm * tk * 2, tk * tn * 2, tm * tn * 4, tm * tn * 4,
                                         tm * tn * 4)),
        name=name,
    )(a, w, residual)


def _merge_kernel(o_ref, w_ref, g_ref, out_ref, acc_ref):
    br = pl.program_id(2)
    a = jnp.concatenate([o_ref[h] for h in range(N_HEAD)], axis=1)
    y = jnp.dot(a, w_ref[...], preferred_element_type=F32)
    contrib = g_ref[...].astype(F32) * y

    @pl.when(br == 0)
    def _():
        acc_ref[...] = contrib

    @pl.when(br > 0)
    def _():
        acc_ref[...] += contrib

    @pl.when(br == N_BRANCH - 1)
    def _():
        out_ref[...] = acc_ref[...].astype(out_ref.dtype)


def _merge(branches, w_br, layer, gates, *, tm, tn):
    m = branches.shape[2]
    n_col = D_MODEL // tn
    return pl.pallas_call(
        _merge_kernel,
        grid=(m // tm, n_col, N_BRANCH),
        in_specs=[pl.BlockSpec((None, N_HEAD, tm, LANES), lambda i, j, g: (g, 0, i, 0)),
                  pl.BlockSpec((None, None, MIX_WIDTH, tn), lambda i, j, g: (layer, g, 0, j)),
                  pl.BlockSpec((tm, tn), lambda i, j, g: (i, g * n_col + j))],
        out_specs=pl.BlockSpec((tm, tn), lambda i, j, g: (i, j)),
        out_shape=jax.ShapeDtypeStruct((m, D_MODEL), BF16),
        scratch_shapes=[pltpu.VMEM((tm, tn), F32)],
        compiler_params=pltpu.CompilerParams(
            dimension_semantics=("arbitrary", "arbitrary", "arbitrary"),
            vmem_limit_bytes=_vmem_limit(tm * MIX_WIDTH * 2, MIX_WIDTH * tn * 2, tm * tn * 2,
                                         tm * tn * 2, tm * tn * 4, tm * MIX_WIDTH * 2)),
        name="branch_merge",
    )(branches, w_br, gates)


def _lower_bound_kernel(p_ref, o_ref):
    p = p_ref[...]
    e = jnp.exp(p - jnp.max(p, axis=0, keepdims=True))
    s = e / jnp.sum(e, axis=0, keepdims=True)
    run = jnp.zeros_like(s[0:1])
    for l in range(DEPTH):
        run = run + s[l:l + 1]
        o_ref[l:l + 1, :] = run - s[0:1]


def _lower_bounds(p):
    return pl.pallas_call(
        _lower_bound_kernel,
        out_shape=jax.ShapeDtypeStruct(p.shape, F32),
        name="hgrn_lower_bounds",
    )(p)


def _gdn_kernel(q_ref, k_ref, v_ref, z_ref, ab_ref, hist_ref, s0_ref, cw_ref, hp_ref, nw_ref,
                buf_ref, o_ref, sout_ref, prev_ref, s_ref, *, chunk):
    del buf_ref
    c = pl.program_id(1)
    C = chunk

    @pl.when(c == 0)
    def _():
        prev_ref[...] = jnp.zeros_like(prev_ref)
        prev_ref[:, :, C - SUBLANES:, :] = hist_ref[0]
        s_ref[...] = s0_ref[0]

    row = lax.broadcasted_iota(jnp.int32, (C, LANES), 0)
    lane = lax.broadcasted_iota(jnp.int32, (C, LANES), 1)
    ii = lax.broadcasted_iota(jnp.int32, (C, C), 0)
    jj = lax.broadcasted_iota(jnp.int32, (C, C), 1)
    eye = ii == jj
    tri = jj <= ii
    strict = jj < ii
    same16 = (ii >> 4) == (jj >> 4)
    eye_f = jnp.where(eye, 1.0, 0.0).astype(F32)

    ab = ab_ref[...]
    g_all = -jnp.exp(hp_ref[0:1, :]) * _softplus(ab + hp_ref[1:2, :])
    beta_all = _sigmoid(ab)
    nw = nw_ref[...]

    def conv(x, xp, w):
        y = x * w[CONV_W - 1:CONV_W, :]
        for j in range(1, CONV_W):
            xs = jnp.where(row < j, pltpu.roll(xp, j, 0), pltpu.roll(x, j, 0))
            y = y + xs * w[CONV_W - 1 - j:CONV_W - j, :]
        return _silu(y)

    def l2n(x):
        return x * lax.rsqrt(jnp.sum(x * x, axis=-1, keepdims=True) + RMS_EPS)

    def load(h):
        raw = [r[h] for r in (q_ref, k_ref, v_ref)]
        prev = [prev_ref[p, h] for p in range(3)]
        taps = [cw_ref[p, h] for p in range(3)]
        return raw, prev, taps, z_ref[h], s_ref[h]

    def store(h, raw, o, s_new):
        for p in range(3):
            prev_ref[p, h] = raw[p]
        s_ref[h] = s_new
        o_ref[h] = o

    def compute(h, raw, prev, taps, z, s):
        q = l2n(conv(raw[0], prev[0], taps[0])) * (HEAD_DIM ** -0.5)
        k = l2n(conv(raw[1], prev[1], taps[1]))
        v = conv(raw[2], prev[2], taps[2])
        g_col = jnp.sum(jnp.where(lane == h, g_all, 0.0), axis=1, keepdims=True)
        b_col = jnp.sum(jnp.where(lane == h + N_HEAD, beta_all, 0.0), axis=1, keepdims=True)

        g_row = jnp.sum(jnp.where(eye, g_col, 0.0), axis=0, keepdims=True)
        gc_col = jnp.sum(jnp.where(tri, g_row, 0.0), axis=1, keepdims=True)
        gc_row = jnp.sum(jnp.where(eye, gc_col, 0.0), axis=0, keepdims=True)
        diff = gc_col - gc_row
        decay = jnp.where(tri, jnp.exp(jnp.where(tri, diff, 0.0)), 0.0)
        egc = jnp.exp(gc_col)
        g_last = gc_col[C - 1:C, :]

        kb = k * b_col
        rhs = jnp.concatenate([v * b_col, kb * egc], axis=1)
        k_dec = k * jnp.exp(g_last - gc_col)
        yield
        qk_kbk = _dot_nt(jnp.concatenate([q, kb], axis=0), k)
        attn = jnp.where(tri, qk_kbk[:C] * decay, 0.0)
        lower = jnp.where(strict, qk_kbk[C:] * decay, 0.0)

        ld = jnp.where(same16, lower, 0.0)
        lo = lower - ld
        yield
        p2 = _dot(ld, ld)
        yield
        p4 = _dot(p2, p2)
        a1 = _dot(eye_f - ld, eye_f + p2)
        yield
        p8 = _dot(p4, p4)
        a2 = _dot(a1, eye_f + p4)
        yield
        td = _dot(a2, eye_f + p8)
        yield
        y_nm = _dot(td, jnp.concatenate([rhs, lo], axis=1))
        y = y_nm[:, :2 * HEAD_DIM]
        nm = y_nm[:, 2 * HEAD_DIM:]
        yield
        n2 = _dot(nm, nm)
        yield
        zz = y + _dot(n2, y)
        yield
        sol = zz - _dot(nm, zz)
        u = sol[:, :HEAD_DIM]
        w = sol[:, HEAD_DIM:]
        yield
        ws_qs = _dot(jnp.concatenate([w, q * egc], axis=0), s)
        v_new = u - ws_qs[:C]
        yield
        o = ws_qs[C:] + _dot(attn, v_new)
        s_new = s * jnp.exp(g_last) + _dot_tn(k_dec, v_new)

        o = o * lax.rsqrt(jnp.mean(o * o, axis=-1, keepdims=True) + RMS_EPS) * nw
        return (o * _silu(z)).astype(o_ref.dtype), s_new

    def group(i, carry):
        heads = [i * GDN_HEADS_PER_STEP + u for u in range(GDN_HEADS_PER_STEP)]
        loaded = [load(h) for h in heads]
        results = _run_interleaved([compute(h, *vals) for h, vals in zip(heads, loaded)])
        for h, vals, (o, s_new) in zip(heads, loaded, results):
            store(h, vals[0], o, s_new)
        return carry

    lax.fori_loop(0, N_HEAD // GDN_HEADS_PER_STEP, group, 0)

    @pl.when(c == pl.num_programs(1) - 1)
    def _():
        sout_ref[0] = s_ref[...]


def _gdn(proj, ab, hist, s0, cw, hp, nw, buf, layer, *, batch, row0, t_pad, chunk):
    nc = t_pad // chunk
    rb0 = row0 // chunk
    l_state = layer if hist.shape[0] > 1 else 0

    def sec(s):
        return pl.BlockSpec((N_HEAD, chunk, LANES), lambda b, c: (s, rb0 + b * nc + c, 0))

    kern = functools.partial(_gdn_kernel, chunk=chunk)
    return pl.pallas_call(
        kern,
        grid=(batch, nc),
        in_specs=[sec(SEC_GQ), sec(SEC_GK), sec(SEC_GV), sec(SEC_GZ),
                  pl.BlockSpec((chunk, LANES), lambda b, c: (rb0 + b * nc + c, 0)),
                  pl.BlockSpec((None, 1, 3, N_HEAD, SUBLANES, LANES),
                               lambda b, c: (l_state, b, 0, 0, 0, 0)),
                  pl.BlockSpec((None, 1, N_HEAD, HEAD_DIM, HEAD_DIM),
                               lambda b, c: (l_state, b, 0, 0, 0)),
                  pl.BlockSpec((None, 3, N_HEAD, SUBLANES, LANES), lambda b, c: (layer, 0, 0, 0, 0)),
                  pl.BlockSpec((None, SUBLANES, LANES), lambda b, c: (layer, 0, 0)),
                  pl.BlockSpec((None, 1, LANES), lambda b, c: (layer, 0, 0)),
                  pl.BlockSpec(memory_space=pl.ANY)],
        out_specs=[pl.BlockSpec((None, N_HEAD, chunk, LANES),
                                lambda b, c: (BR_GDN, 0, rb0 + b * nc + c, 0)),
                   pl.BlockSpec((1, N_HEAD, HEAD_DIM, HEAD_DIM), lambda b, c: (b, 0, 0, 0))],
        out_shape=[jax.ShapeDtypeStruct(buf.shape, buf.dtype),
                   jax.ShapeDtypeStruct((batch, N_HEAD, HEAD_DIM, HEAD_DIM), F32)],
        input_output_aliases={10: 0},
        scratch_shapes=[pltpu.VMEM((3, N_HEAD, chunk, LANES), F32),
                        pltpu.VMEM((N_HEAD, HEAD_DIM, HEAD_DIM), F32)],
        compiler_params=pltpu.CompilerParams(
            dimension_semantics=("arbitrary", "arbitrary"),
            vmem_limit_bytes=32 * 1024 * 1024),
        name="gdn_mixer",
    )(proj, proj, proj, proj, ab, hist, s0, cw, hp, nw, buf)


def _hgrn_kernel(q_ref, f_ref, i_ref, g_ref, lb_ref, s0_ref, nw_ref, buf_ref, o_ref, sout_ref,
                 s_ref, *, chunk):
    del buf_ref
    c = pl.program_id(1)
    C = chunk
    halves = [r for r in (32, 16, 8) if 2 * r <= C]

    @pl.when(c == 0)
    def _():
        s_ref[...] = s0_ref[0]

    ii = lax.broadcasted_iota(jnp.int32, (C, C), 0)
    jj = lax.broadcasted_iota(jnp.int32, (C, C), 1)
    sel_blocks = [jnp.where(jj <= ii, 1.0, 0.0)]
    level_masks = []
    for r in halves:
        ref_row = (ii & ~(2 * r - 1)) + (r - 1)
        sel_blocks.append(jnp.where(jj <= ref_row, 1.0, 0.0))
        same_pair = (ii & ~(2 * r - 1)) == (jj & ~(2 * r - 1))
        level_masks.append(same_pair & ((ii & r) != 0) & ((jj & r) == 0))
    sel = jnp.concatenate(sel_blocks, axis=0).astype(BF16)
    nw = nw_ref[...]
    row8 = lax.broadcasted_iota(jnp.int32, (SUBLANES, LANES), 0)

    def load(h):
        return q_ref[h], f_ref[h], i_ref[h], g_ref[h], lb_ref[h], s_ref[h]

    def compute(q_raw, f_raw, v, g_raw, lb, st):
        q = _silu(q_raw)
        f = lb + (1.0 - lb) * _sigmoid(f_raw)
        logf = jnp.log(f)
        k = 1.0 - f
        yield
        cums = _dot_sel_lhs(sel, logf)
        b = cums[:C]
        q_dec = q * jnp.exp(b)
        b_last = b[C - 1:C]
        k_dec = k * jnp.exp(b_last - b)
        yield
        attn = jnp.zeros((C, C), F32)
        for l, mask in enumerate(level_masks):
            bref = cums[(l + 1) * C:(l + 2) * C]
            qt = q * jnp.exp(jnp.minimum(b - bref, 0.0))
            kt = k * jnp.exp(jnp.minimum(bref - b, 0.0))
            attn = attn + jnp.where(mask, _dot_nt(qt, kt), 0.0)
        o_inter = _dot_nt(q_dec, st)
        st_new = st * jnp.exp(b_last) + _dot_tn(v, k_dec)
        yield
        o = o_inter + _dot(attn, v)

        diag = []
        for d in range(C // SUBLANES):
            sl = slice(d * SUBLANES, (d + 1) * SUBLANES)
            qd, kd, bd, vd = q[sl], k[sl], b[sl], v[sl]
            od = jnp.zeros((SUBLANES, LANES), F32)
            for s in range(SUBLANES):
                e = jnp.exp(jnp.minimum(bd - bd[s:s + 1], 0.0))
                wgt = jnp.sum(qd * kd[s:s + 1] * e, axis=1, keepdims=True)
                od = od + jnp.where(row8 >= s, wgt, 0.0) * vd[s:s + 1]
            diag.append(od)
        o = o + jnp.concatenate(diag, axis=0)

        o = o * lax.rsqrt(jnp.mean(o * o, axis=-1, keepdims=True) + RMS_EPS) * nw
        return (o * _silu(g_raw)).astype(o_ref.dtype), st_new

    def group(i, carry):
        heads = [i * HGRN_HEADS_PER_STEP + u for u in range(HGRN_HEADS_PER_STEP)]
        results = _run_interleaved([compute(*vals) for vals in [load(h) for h in heads]])
        for h, (o, st_new) in zip(heads, results):
            s_ref[h] = st_new
            o_ref[h] = o
        return carry

    lax.fori_loop(0, N_HEAD // HGRN_HEADS_PER_STEP, group, 0)

    @pl.when(c == pl.num_programs(1) - 1)
    def _():
        sout_ref[0] = s_ref[...]


def _hgrn(proj, lb, s0_t, nw, buf, layer, *, batch, row0, t_pad, chunk):
    nc = t_pad // chunk
    rb0 = row0 // chunk
    l_state = layer if s0_t.shape[0] > 1 else 0

    def sec(s):
        return pl.BlockSpec((N_HEAD, chunk, LANES), lambda b, c: (s, rb0 + b * nc + c, 0))

    kern = functools.partial(_hgrn_kernel, chunk=chunk)
    return pl.pallas_call(
        kern,
        grid=(batch, nc),
        in_specs=[sec(SEC_HQ), sec(SEC_HF), sec(SEC_HI), sec(SEC_HG),
                  pl.BlockSpec((None, N_HEAD, 1, LANES), lambda b, c: (layer, 0, 0, 0)),
                  pl.BlockSpec((None, 1, N_HEAD, HEAD_DIM, HEAD_DIM),
                               lambda b, c: (l_state, b, 0, 0, 0)),
                  pl.BlockSpec((None, 1, LANES), lambda b, c: (layer, 0, 0)),
                  pl.BlockSpec(memory_space=pl.ANY)],
        out_specs=[pl.BlockSpec((None, N_HEAD, chunk, LANES),
                                lambda b, c: (BR_HGRN, 0, rb0 + b * nc + c, 0)),
                   pl.BlockSpec((1, N_HEAD, HEAD_DIM, HEAD_DIM), lambda b, c: (b, 0, 0, 0))],
        out_shape=[jax.ShapeDtypeStruct(buf.shape, buf.dtype),
                   jax.ShapeDtypeStruct((batch, N_HEAD, HEAD_DIM, HEAD_DIM), F32)],
        input_output_aliases={7: 0},
        scratch_shapes=[pltpu.VMEM((N_HEAD, HEAD_DIM, HEAD_DIM), F32)],
        compiler_params=pltpu.CompilerParams(
            dimension_semantics=("arbitrary", "arbitrary"),
            vmem_limit_bytes=32 * 1024 * 1024),
        name="hgrn_mixer",
    )(proj, proj, proj, proj, lb, s0_t, nw, buf)


def _sb_kernel(*refs, q_block, t_new, t_past, pad_front):
    if t_past:
        q_ref, kn_ref, vn_ref, kp_ref, vp_ref, _, o_ref, ko_ref, vo_ref, kb_ref, vb_ref = refs
    else:
        q_ref, kn_ref, vn_ref, _, o_ref, ko_ref, vo_ref, kb_ref, vb_ref = refs
    qi = pl.program_id(2)
    t_all = t_past + t_new
    t_scr = kb_ref.shape[0]
    n_sub = SB_KEY_SLAB // LANES

    @pl.when(qi == 0)
    def _():
        k_new = kn_ref[0]
        v_new = vn_ref[0]
        ko_ref[0] = k_new[pad_front:, :]
        vo_ref[0] = v_new[pad_front:, :]
        if t_past:
            kb_ref[0:t_past, :] = kp_ref[0].astype(BF16)
            vb_ref[0:t_past, :] = vp_ref[0].astype(BF16)
        kb_ref[t_past:t_all, :] = k_new.astype(BF16)
        vb_ref[t_past:t_all, :] = v_new.astype(BF16)
        if t_scr > t_all:
            kb_ref[t_all:, :] = jnp.zeros((t_scr - t_all, LANES), BF16)
            vb_ref[t_all:, :] = jnp.zeros((t_scr - t_all, LANES), BF16)

    n_part = 2 if q_block % (4 * SUBLANES * 2) == 0 and q_block >= LANES else 1
    rows = q_block // n_part

    q = q_ref[0].astype(BF16)
    q_pos = t_past + qi * q_block + lax.broadcasted_iota(jnp.int32, (rows, SB_KEY_SLAB), 0)
    lane = lax.broadcasted_iota(jnp.int32, (rows, SB_KEY_SLAB), 1)
    jr = lax.broadcasted_iota(jnp.int32, (LANES, 2 * LANES), 0)
    sc = lax.broadcasted_iota(jnp.int32, (LANES, 2 * LANES), 1)
    suffix_and_total = jnp.where((sc >= LANES) | (jr > sc), 1.0, 0.0).astype(BF16)
    slab_hi = (t_past + qi * q_block + q_block - 1) // SB_KEY_SLAB
    scale = HEAD_DIM ** -0.5

    def cond(carry):
        it, _, _, active = carry
        return jnp.logical_and(it <= slab_hi, active > 0)

    def part(p, slab, kt, vt, run, acc):
        sl = slice(p * rows, (p + 1) * rows)
        z = lax.dot_general(q[sl], kt, (((1,), (1,)), ((), ())), preferred_element_type=F32) * scale
        yield
        k_pos = slab * SB_KEY_SLAB + lane
        vis = (k_pos < q_pos + p * rows) & (k_pos >= pad_front)
        sp = _softplus(z)
        log_keep = jnp.where(vis, -sp, 0.0)
        between = [None] * n_sub
        for j in reversed(range(n_sub)):
            sums = _dot_sel_rhs(log_keep[:, j * LANES:(j + 1) * LANES], suffix_and_total)
            between[j] = sums[:, :LANES] + run
            run = run + sums[:, LANES:]
        yield
        a = jnp.where(vis, jnp.exp((z - sp) + jnp.concatenate(between, axis=1)), 0.0)
        acc = acc + jnp.dot(a.astype(BF16), vt, preferred_element_type=F32)
        return run, acc

    def body(carry):
        it, run, acc, _ = carry
        slab = slab_hi - it
        start = pl.multiple_of(slab * SB_KEY_SLAB, SB_KEY_SLAB)
        kt = kb_ref[pl.ds(start, SB_KEY_SLAB), :]
        vt = vb_ref[pl.ds(start, SB_KEY_SLAB), :]
        done = _run_interleaved([
            part(p, slab, kt, vt, run[p * rows:(p + 1) * rows], acc[p * rows:(p + 1) * rows])
            for p in range(n_part)])
        run = jnp.concatenate([r for r, _ in done], axis=0)
        acc = jnp.concatenate([a for _, a in done], axis=0)
        return it + 1, run, acc, (jnp.max(run) > EXP_UNDERFLOW).astype(jnp.int32)

    zero = jnp.zeros((q_block, LANES), F32)
    _, _, acc, _ = lax.while_loop(cond, body, (jnp.int32(0), zero, zero, jnp.int32(1)))
    o_ref[...] = acc.astype(o_ref.dtype)


def _sb(proj, past_k, past_v, buf, layer, *, batch, row0, t_pad, q_block, t_past, pad_front):
    nq = t_pad // q_block
    rbq0 = row0 // q_block
    rbk0 = row0 // t_pad
    t_scr = -(-(t_past + t_pad) // SB_KEY_SLAB) * SB_KEY_SLAB
    t_out = t_pad - pad_front
    sec_q = SEC_SQ * N_HEAD
    sec_k = SEC_SK * N_HEAD
    sec_v = SEC_SV * N_HEAD
    in_specs = [pl.BlockSpec((1, q_block, LANES), lambda b, h, i: (sec_q + h, rbq0 + b * nq + i, 0)),
                pl.BlockSpec((1, t_pad, LANES), lambda b, h, i: (sec_k + h, rbk0 + b, 0)),
                pl.BlockSpec((1, t_pad, LANES), lambda b, h, i: (sec_v + h, rbk0 + b, 0))]
    args = [proj, proj, proj]
    if t_past:
        in_specs += [pl.BlockSpec((None, 1, t_past, LANES), lambda b, h, i: (layer, b, 0, h)),
                     pl.BlockSpec((None, 1, t_past, LANES), lambda b, h, i: (layer, b, 0, h))]
        args += [past_k, past_v]
    in_specs.append(pl.BlockSpec(memory_space=pl.ANY))
    args.append(buf)
    kv_out = pl.BlockSpec((1, t_out, LANES), lambda b, h, i: (b, 0, h))
    kern = functools.partial(_sb_kernel, q_block=q_block, t_new=t_pad, t_past=t_past,
                             pad_front=pad_front)
    return pl.pallas_call(
        kern,
        grid=(batch, N_HEAD, nq),
        in_specs=in_specs,
        out_specs=[pl.BlockSpec((None, None, q_block, LANES),
                                lambda b, h, i: (BR_SB, h, rbq0 + b * nq + i, 0)),
                   kv_out, kv_out],
        input_output_aliases={len(args) - 1: 0},
        out_shape=[jax.ShapeDtypeStruct(buf.shape, buf.dtype),
                   jax.ShapeDtypeStruct((batch, t_out, MIX_WIDTH), F32),
                   jax.ShapeDtypeStruct((batch, t_out, MIX_WIDTH), F32)],
        scratch_shapes=[pltpu.VMEM((t_scr, LANES), BF16), pltpu.VMEM((t_scr, LANES), BF16)],
        compiler_params=pltpu.CompilerParams(
            dimension_semantics=("arbitrary", "arbitrary", "arbitrary"),
            vmem_limit_bytes=32 * 1024 * 1024),
        name="stickbreak_attention",
    )(*args)


def kernel(x_prompt, x_sample, cache_sb_k, cache_sb_v, state_gdn_conv, state_gdn, state_hgrn,
           meta_tokens, norm1_w, w_in, gdn_conv_w, gdn_a_log, gdn_dt_bias, gdn_norm_w,
           hgrn_lb_param, hgrn_norm_w, w_branch, w_o, norm2_w, w_ff_gate, w_ff_up, w_ff_down,
           final_norm_w):
    b_p, seq, _ = x_prompt.shape
    b_s, t_s, _ = x_sample.shape
    t_p = N_META + seq
    sb_qb = SB_Q_BLOCK_P if t_p > SB_Q_BLOCK_P else CHUNK_P
    t_pp = -(-t_p // sb_qb) * sb_qb
    pad_front = t_pp - t_p
    m_p = b_p * t_pp
    m_s = b_s * t_s
    m_all = m_p + m_s
    past_len = cache_sb_k.shape[2]
    tm = _row_tile(m_all, 1088)
    t_skip = pad_front + N_META
    assert t_pp % CHUNK_P == 0 and pad_front % SUBLANES == 0 and t_skip % CHUNK_P == 0

    meta = jnp.broadcast_to(meta_tokens[None], (b_p, N_META, D_MODEL)).astype(F32)
    xp = jnp.concatenate([jnp.zeros((b_p, pad_front, D_MODEL), F32), meta, x_prompt], axis=1)
    x = jnp.concatenate([xp.reshape(m_p, D_MODEL), x_sample.reshape(m_s, D_MODEL)], axis=0)

    w_in_a = w_in[:, :, :_W_IN_AB].astype(BF16)
    w_in_b = w_in[:, :, _W_IN_SB:_W_IN_GATE].astype(BF16)
    w_in_g = w_in[:, :, _W_IN_GATE:].astype(BF16)
    w_in_ab = jnp.pad(w_in[:, :, _W_IN_AB:_W_IN_SB],
                      ((0, 0), (0, 0), (0, LANES - 2 * N_HEAD))).astype(BF16)
    w_br = w_branch.astype(BF16)
    w_out = w_o.astype(BF16)
    pad_ff = D_FF_PAD - D_FF
    w_fg = jnp.pad(w_ff_gate, ((0, 0), (0, 0), (0, pad_ff))).astype(BF16)
    w_fu = jnp.pad(w_ff_up, ((0, 0), (0, 0), (0, pad_ff))).astype(BF16)
    w_fd = jnp.pad(w_ff_down, ((0, 0), (0, pad_ff), (0, 0))).astype(BF16)

    lower_bounds = _lower_bounds(hgrn_lb_param).reshape(DEPTH, N_HEAD, 1, HEAD_DIM)
    cw = gdn_conv_w.reshape(DEPTH, CONV_W, 3, N_HEAD, HEAD_DIM).transpose(0, 2, 3, 1, 4)
    cw = jnp.pad(cw, ((0, 0), (0, 0), (0, 0), (0, SUBLANES - CONV_W), (0, 0)))
    hp = jnp.zeros((DEPTH, SUBLANES, LANES), F32)
    hp = hp.at[:, 0, :N_HEAD].set(gdn_a_log).at[:, 1, :N_HEAD].set(gdn_dt_bias)
    gnw = gdn_norm_w.reshape(DEPTH, 1, HEAD_DIM)
    hnw = hgrn_norm_w.reshape(DEPTH, 1, HEAD_DIM)
    hist_s = state_gdn_conv.reshape(DEPTH, b_s, CONV_W - 1, 3, N_HEAD, HEAD_DIM)
    hist_s = jnp.pad(hist_s.transpose(0, 1, 3, 4, 2, 5),
                     ((0, 0),) * 4 + ((SUBLANES - (CONV_W - 1), 0), (0, 0)))
    zero_hist = jnp.zeros((1, b_p, 3, N_HEAD, SUBLANES, LANES), F32)
    zero_state = jnp.zeros((1, b_p, N_HEAD, HEAD_DIM, HEAD_DIM), F32)
    hgrn_s0 = state_hgrn.swapaxes(-1, -2)
    past_k = cache_sb_k.reshape(DEPTH, b_s, past_len, MIX_WIDTH)
    past_v = cache_sb_v.reshape(DEPTH, b_s, past_len, MIX_WIDTH)
    branches = jnp.zeros((N_BRANCH, N_HEAD, m_all, LANES), BF16)

    outs = {name: [] for name in ("kp", "vp", "ks", "vs", "cp", "cs", "gp", "gs", "hp", "hs")}
    for l in range(DEPTH):
        h = _rmsnorm(x, norm1_w[l], BF16)
        proj_a = _dense(h, w_in_a, l, tm=tm, tn=1024, mode="headmajor", name="proj_gdn")
        proj_b = _dense(h, w_in_b, l, tm=tm, tn=1024, mode="headmajor", name="proj_sb_hgrn")
        ab = _dense(h, w_in_ab, l, tm=tm, tn=LANES, name="proj_ab")
        gates = _dense(h, w_in_g, l, tm=tm, tn=1024, mode="sigmoid", out_dtype=BF16, name="proj_gates")

        branches, gs_p = _gdn(proj_a, ab, zero_hist, zero_state, cw, hp, gnw, branches, l,
                              batch=b_p, row0=0, t_pad=t_pp, chunk=CHUNK_P)
        branches, gs_s = _gdn(proj_a, ab, hist_s, state_gdn, cw, hp, gnw, branches, l,
                              batch=b_s, row0=m_p, t_pad=t_s, chunk=t_s)
        branches, k_p, v_p = _sb(proj_b, None, None, branches, l, batch=b_p, row0=0, t_pad=t_pp,
                                 q_block=sb_qb, t_past=0, pad_front=pad_front)
        branches, k_s, v_s = _sb(proj_b, past_k, past_v, branches, l, batch=b_s, row0=m_p,
                                 t_pad=t_s, q_block=t_s, t_past=past_len, pad_front=0)
        branches, hs_p = _hgrn(proj_b, lower_bounds, zero_state, hnw, branches, l,
                               batch=b_p, row0=0, t_pad=t_pp, chunk=CHUNK_P)
        branches, hs_s = _hgrn(proj_b, lower_bounds, hgrn_s0, hnw, branches, l,
                               batch=b_s, row0=m_p, t_pad=t_s, chunk=t_s)

        merged = _merge(branches, w_br, l, gates, tm=tm, tn=1024)
        x = _dense(merged, w_out, l, tm=tm, tn=1024, mode="residual", residual=x, name="out_proj")

        h2 = _rmsnorm(x, norm2_w[l], BF16)
        hmid = _dense(h2, w_fg, l, tm=tm, tn=512, mode="swiglu", out_dtype=BF16, w2=w_fu,
                      name="ffn_gate_up")
        x = _dense_ktiled_residual(hmid, w_fd, l, x, tm=tm, tn=1024, tk=1408, name="ffn_down")

        outs["kp"].append(k_p.reshape(b_p, t_p, N_HEAD, HEAD_DIM))
        outs["vp"].append(v_p.reshape(b_p, t_p, N_HEAD, HEAD_DIM))
        outs["ks"].append(k_s.reshape(b_s, t_s, N_HEAD, HEAD_DIM))
        outs["vs"].append(v_s.reshape(b_s, t_s, N_HEAD, HEAD_DIM))
        n_qkv = 3 * N_HEAD
        qkv_p = proj_a[:n_qkv, :m_p].reshape(n_qkv, b_p, t_pp, HEAD_DIM)[:, :, t_pp - (CONV_W - 1):]
        outs["cp"].append(qkv_p.transpose(1, 2, 0, 3).reshape(b_p, CONV_W - 1, 3 * MIX_WIDTH))
        qkv_s = proj_a[:n_qkv, m_p:].reshape(n_qkv, b_s, t_s, HEAD_DIM)[:, :, t_s - (CONV_W - 1):]
        outs["cs"].append(qkv_s.transpose(1, 2, 0, 3).reshape(b_s, CONV_W - 1, 3 * MIX_WIDTH))
        outs["gp"].append(gs_p)
        outs["gs"].append(gs_s)
        outs["hp"].append(hs_p.swapaxes(-1, -2))
        outs["hs"].append(hs_s.swapaxes(-1, -2))

    tf = CHUNK_P
    y_prompt = _rmsnorm(
        x, final_norm_w, F32, tm=tf, grid=(b_p, seq // tf),
        in_map=lambda b, i: (b * (t_pp // tf) + t_skip // tf + i, 0),
        out_map=lambda b, i: (b * (seq // tf) + i, 0), out_rows=b_p * seq).reshape(b_p, seq, D_MODEL)
    ts = _row_tile(m_s, 256)
    y_sample = _rmsnorm(
        x, final_norm_w, F32, tm=ts, grid=(m_s // ts,),
        in_map=lambda i: (m_p // ts + i, 0), out_map=lambda i: (i, 0),
        out_rows=m_s).reshape(b_s, t_s, D_MODEL)
    st = {k: jnp.stack(v) for k, v in outs.items()}
    return (y_prompt, y_sample, st["kp"], st["vp"], st["ks"], st["vs"], st["cp"], st["cs"],
            st["gp"], st["gs"], st["hp"], st["hs"])
```

```python
import functools

import jax
import jax.numpy as jnp
from jax import lax
from jax.experimental import pallas as pl
from jax.experimental.pallas import tpu as pltpu

F32 = jnp.float32
BF16 = jnp.bfloat16

D_MODEL = 4096
DEPTH = 4
N_HEAD = 16
HEAD_DIM = 128
MIX_WIDTH = N_HEAD * HEAD_DIM
N_META = 16
CONV_W = 4
D_FF = 11008
RMS_EPS = 1e-6
N_BRANCH = 3

LANES = 128
SUBLANES = 8
VMEM_LIMIT_CAP = 60 * 1024 * 1024

CHUNK_P = 64
SB_Q_BLOCK_P = 192
SB_KEY_SLAB = 3 * LANES
D_FF_PAD = -(-D_FF // 1024) * 1024
GDN_HEADS_PER_STEP = 16
HGRN_HEADS_PER_STEP = 4
EXP_UNDERFLOW = -104.0
SEC_GQ, SEC_GK, SEC_GV, SEC_GZ = range(4)
SEC_SQ, SEC_SK, SEC_SV, SEC_HQ, SEC_HF, SEC_HI, SEC_HG = range(7)
BR_GDN, BR_SB, BR_HGRN = range(3)
_W_IN_AB = 4 * MIX_WIDTH
_W_IN_SB = _W_IN_AB + 2 * N_HEAD
_W_IN_GATE = _W_IN_SB + 7 * MIX_WIDTH


def _vmem_limit(*block_bytes):
    need = 2 * sum(block_bytes) + 2 * max(block_bytes)
    return int(min(max(need, 16 * 1024 * 1024), VMEM_LIMIT_CAP))


def _sigmoid(x):
    return 1.0 / (1.0 + jnp.exp(-x))


def _silu(x):
    return x * _sigmoid(x)


def _softplus(x):
    return jnp.maximum(x, 0.0) + jnp.log1p(jnp.exp(-jnp.abs(x)))


def _dot(a, b):
    return jnp.dot(a.astype(BF16), b.astype(BF16), preferred_element_type=F32)


def _dot_nt(a, b):
    return lax.dot_general(a.astype(BF16), b.astype(BF16), (((1,), (1,)), ((), ())),
                           preferred_element_type=F32)


def _dot_tn(a, b):
    return lax.dot_general(a.astype(BF16), b.astype(BF16), (((0,), (0,)), ((), ())),
                           preferred_element_type=F32)


def _split(x):
    hi = x.astype(BF16)
    lo = (x - hi.astype(F32)).astype(BF16)
    return hi, lo


def _dot_sel_lhs(sel, x):
    hi, lo = _split(x)
    d = functools.partial(jnp.dot, preferred_element_type=F32)
    return d(sel, hi) + d(sel, lo)


def _dot_sel_rhs(x, sel):
    hi, lo = _split(x)
    d = functools.partial(jnp.dot, preferred_element_type=F32)
    return d(hi, sel) + d(lo, sel)


def _run_interleaved(gens):
    results = [None] * len(gens)
    live = list(range(len(gens)))
    while live:
        for i in list(live):
            try:
                next(gens[i])
            except StopIteration as done:
                results[i] = done.value
                live.remove(i)
    return results


def _row_tile(m, cap):
    return max(t for t in range(16, cap + 1, 16) if m % t == 0)


def _cast_kernel(x_ref, o_ref, *, pad_axis, n_real):
    if pad_axis is None:
        o_ref[...] = x_ref[...].astype(o_ref.dtype)
        return
    i = pl.program_id(pad_axis)

    @pl.when(i < n_real)
    def _():
        o_ref[...] = x_ref[...].astype(o_ref.dtype)

    @pl.when(i >= n_real)
    def _():
        o_ref[...] = jnp.zeros_like(o_ref)


def _cast_bf16(w, *, block, out_rows=None, out_cols=None):
    n_l, n_r, n_c = w.shape
    br, bc = block
    out_rows = n_r if out_rows is None else out_rows
    out_cols = n_c if out_cols is None else out_cols
    pad_axis, n_real = None, 0
    if out_rows > n_r:
        assert n_r % br == 0 and out_cols <= n_c
        pad_axis, n_real = 1, n_r // br
    elif out_cols > n_c:
        assert n_c % bc == 0
        pad_axis, n_real = 2, n_c // bc
    last_r = -(-n_r // br) - 1
    last_c = -(-n_c // bc) - 1
    kern = functools.partial(_cast_kernel, pad_axis=pad_axis, n_real=n_real)
    return pl.pallas_call(
        kern,
        grid=(n_l, out_rows // br, out_cols // bc),
        in_specs=[pl.BlockSpec((None, br, bc),
                               lambda l, i, j: (l, jnp.minimum(i, last_r), jnp.minimum(j, last_c)))],
        out_specs=pl.BlockSpec((None, br, bc), lambda l, i, j: (l, i, j)),
        out_shape=jax.ShapeDtypeStruct((n_l, out_rows, out_cols), BF16),
        compiler_params=pltpu.CompilerParams(
            dimension_semantics=("arbitrary",) * 3,
            vmem_limit_bytes=_vmem_limit(br * bc * 4, br * bc * 2)),
        name="cast_bf16",
    )(w)


def _cast_shift_kernel(a_ref, b_ref, o_ref, *, shift):
    tail = jnp.concatenate([b_ref[:, :shift],
                            jnp.zeros((b_ref.shape[0], LANES - shift), b_ref.dtype)], axis=1)
    both = jnp.concatenate([a_ref[...], tail], axis=1)
    width = both.shape[1]
    o_ref[...] = pltpu.roll(both, width - shift, 1)[:, :o_ref.shape[1]].astype(o_ref.dtype)


def _cast_bf16_shifted(w, *, col0, n_cols, block):
    n_l, n_r, _ = w.shape
    br, bc = block
    shift = col0 % LANES
    base = col0 - shift
    assert shift and base % bc == 0 and n_cols % bc == 0 and n_r % br == 0
    kern = functools.partial(_cast_shift_kernel, shift=shift)
    return pl.pallas_call(
        kern,
        grid=(n_l, n_r // br, n_cols // bc),
        in_specs=[pl.BlockSpec((None, br, bc), lambda l, i, j: (l, i, base // bc + j)),
                  pl.BlockSpec((None, br, LANES),
                               lambda l, i, j: (l, i, (base + (j + 1) * bc) // LANES))],
        out_specs=pl.BlockSpec((None, br, bc), lambda l, i, j: (l, i, j)),
        out_shape=jax.ShapeDtypeStruct((n_l, n_r, n_cols), BF16),
        compiler_params=pltpu.CompilerParams(
            dimension_semantics=("arbitrary",) * 3,
            vmem_limit_bytes=_vmem_limit(br * bc * 4, br * bc * 4, br * bc * 2)),
        name="cast_bf16_shifted",
    )(w, w)


def _rmsnorm_kernel(x_ref, w_ref, o_ref):
    x = x_ref[...]
    ms = jnp.mean(x * x, axis=-1, keepdims=True)
    o_ref[...] = (x * lax.rsqrt(ms + RMS_EPS) * w_ref[...]).astype(o_ref.dtype)


def _rmsnorm(x, w, out_dtype, *, tm=None, grid=None, in_map=None, out_map=None, out_rows=None):
    m, d = x.shape
    if tm is None:
        tm = _row_tile(m, 256)
    if grid is None:
        grid, in_map, out_map, out_rows = (m // tm,), (lambda i: (i, 0)), (lambda i: (i, 0)), m
    return pl.pallas_call(
        _rmsnorm_kernel,
        grid=grid,
        in_specs=[pl.BlockSpec((tm, d), in_map),
                  pl.BlockSpec((1, d), lambda *g: (0, 0))],
        out_specs=pl.BlockSpec((tm, d), out_map),
        out_shape=jax.ShapeDtypeStruct((out_rows, d), out_dtype),
        compiler_params=pltpu.CompilerParams(
            dimension_semantics=("arbitrary",) * len(grid),
            vmem_limit_bytes=_vmem_limit(tm * d * 4, tm * d * 4)),
        name="rmsnorm",
    )(x, w.reshape(1, d))


def _mm_kernel(a_ref, w_ref, o_ref):
    o_ref[...] = jnp.dot(a_ref[...], w_ref[...], preferred_element_type=F32).astype(o_ref.dtype)


def _mm_headmajor_kernel(a_ref, w_ref, o_ref):
    y = jnp.dot(a_ref[...], w_ref[...], preferred_element_type=F32)
    for j in range(o_ref.shape[0]):
        o_ref[j] = y[:, j * LANES:(j + 1) * LANES].astype(o_ref.dtype)


def _mm_sigmoid_kernel(a_ref, w_ref, o_ref):
    y = jnp.dot(a_ref[...], w_ref[...], preferred_element_type=F32)
    o_ref[...] = _sigmoid(y).astype(o_ref.dtype)


def _mm_residual_kernel(a_ref, w_ref, r_ref, o_ref):
    y = jnp.dot(a_ref[...], w_ref[...], preferred_element_type=F32)
    o_ref[...] = r_ref[...] + y


def _mm_swiglu_kernel(a_ref, wg_ref, wu_ref, o_ref):
    a = a_ref[...]
    g = jnp.dot(a, wg_ref[...], preferred_element_type=F32)
    u = jnp.dot(a, wu_ref[...], preferred_element_type=F32)
    o_ref[...] = (_silu(g) * u).astype(o_ref.dtype)


def _dense(a, w, layer, *, tm, tn, mode="plain", out_dtype=F32, residual=None, w2=None,
           col0=0, n=None, name="dense"):
    m, k = a.shape
    n = w.shape[2] - col0 if n is None else n
    assert col0 % tn == 0 and n % tn == 0
    grid = (m // tm, n // tn)
    a_spec = pl.BlockSpec((tm, k), lambda i, j: (i, 0))
    w_spec = pl.BlockSpec((None, k, tn), lambda i, j: (layer, 0, col0 // tn + j))
    o_spec = pl.BlockSpec((tm, tn), lambda i, j: (i, j))
    out_shape = jax.ShapeDtypeStruct((m, n), out_dtype)
    out_bytes = tm * tn * jnp.dtype(out_dtype).itemsize
    blocks = [tm * k * 2, k * tn * 2, out_bytes, tm * tn * 4]
    if mode == "plain":
        kern, ins, specs = _mm_kernel, (a, w), [a_spec, w_spec]
    elif mode == "headmajor":
        kern, ins, specs = _mm_headmajor_kernel, (a, w), [a_spec, w_spec]
        o_spec = pl.BlockSpec((tn // LANES, tm, LANES), lambda i, j: (j, i, 0))
        out_shape = jax.ShapeDtypeStruct((n // LANES, m, LANES), out_dtype)
    elif mode == "sigmoid":
        kern, ins, specs = _mm_sigmoid_kernel, (a, w), [a_spec, w_spec]
    elif mode == "residual":
        kern, ins, specs = _mm_residual_kernel, (a, w, residual), [a_spec, w_spec, o_spec]
        blocks.append(tm * tn * 4)
    elif mode == "swiglu":
        kern, ins, specs = _mm_swiglu_kernel, (a, w, w2), [a_spec, w_spec, w_spec]
        blocks += [k * tn * 2, tm * tn * 4]
    else:
        raise ValueError(mode)
    return pl.pallas_call(
        kern, grid=grid, in_specs=specs, out_specs=o_spec, out_shape=out_shape,
        compiler_params=pltpu.CompilerParams(
            dimension_semantics=("arbitrary", "arbitrary"),
            vmem_limit_bytes=_vmem_limit(*blocks)),
        name=name,
    )(*ins)


def _mm_ktiled_kernel(a_ref, w_ref, r_ref, o_ref, acc_ref):
    kk = pl.program_id(2)
    y = jnp.dot(a_ref[...], w_ref[...], preferred_element_type=F32)

    @pl.when(kk == 0)
    def _():
        acc_ref[...] = r_ref[...] + y

    @pl.when(kk > 0)
    def _():
        acc_ref[...] += y

    @pl.when(kk == pl.num_programs(2) - 1)
    def _():
        o_ref[...] = acc_ref[...]


def _dense_ktiled_residual(a, w, layer, residual, *, tm, tn, tk, name):
    m, k = a.shape
    n = w.shape[2]
    return pl.pallas_call(
        _mm_ktiled_kernel,
        grid=(m // tm, n // tn, k // tk),
        in_specs=[pl.BlockSpec((tm, tk), lambda i, j, kk: (i, kk)),
                  pl.BlockSpec((None, tk, tn), lambda i, j, kk: (layer, kk, j)),
                  pl.BlockSpec((tm, tn), lambda i, j, kk: (i, j))],
        out_specs=pl.BlockSpec((tm, tn), lambda i, j, kk: (i, j)),
        out_shape=jax.ShapeDtypeStruct((m, n), F32),
        scratch_shapes=[pltpu.VMEM((tm, tn), F32)],
        compiler_params=pltpu.CompilerParams(
            dimension_semantics=("arbitrary", "arbitrary", "arbitrary"),
            vmem_limit_bytes=_vmem_limit(tm * tk * 2, tk * tn * 2, tm * tn * 4, tm * tn * 4,
                                         tm * tn * 4)),
        name=name,
    )(a, w, residual)


def _merge_kernel(o_ref, w_ref, g_ref, out_ref, acc_ref):
    br = pl.program_id(2)
    a = jnp.concatenate([o_ref[h] for h in range(N_HEAD)], axis=1)
    y = jnp.dot(a, w_ref[...], preferred_element_type=F32)
    contrib = g_ref[...].astype(F32) * y

    @pl.when(br == 0)
    def _():
        acc_ref[...] = contrib

    @pl.when(br > 0)
    def _():
        acc_ref[...] += contrib

    @pl.when(br == N_BRANCH - 1)
    def _():
        out_ref[...] = acc_ref[...].astype(out_ref.dtype)


def _merge(branches, w_br, layer, gates, *, tm, tn):
    m = branches.shape[2]
    n_col = D_MODEL // tn
    return pl.pallas_call(
        _merge_kernel,
        grid=(m // tm, n_col, N_BRANCH),
        in_specs=[pl.BlockSpec((None, N_HEAD, tm, LANES), lambda i, j, g: (g, 0, i, 0)),
                  pl.BlockSpec((None, MIX_WIDTH, tn), lambda i, j, g: (layer * N_BRANCH + g, 0, j)),
                  pl.BlockSpec((tm, tn), lambda i, j, g: (i, g * n_col + j))],
        out_specs=pl.BlockSpec((tm, tn), lambda i, j, g: (i, j)),
        out_shape=jax.ShapeDtypeStruct((m, D_MODEL), BF16),
        scratch_shapes=[pltpu.VMEM((tm, tn), F32)],
        compiler_params=pltpu.CompilerParams(
            dimension_semantics=("arbitrary", "arbitrary", "arbitrary"),
            vmem_limit_bytes=_vmem_limit(tm * MIX_WIDTH * 2, MIX_WIDTH * tn * 2, tm * tn * 2,
                                         tm * tn * 2, tm * tn * 4, tm * MIX_WIDTH * 2)),
        name="branch_merge",
    )(branches, w_br, gates)


def _lower_bound_kernel(p_ref, o_ref):
    p = p_ref[...]
    e = jnp.exp(p - jnp.max(p, axis=0, keepdims=True))
    s = e / jnp.sum(e, axis=0, keepdims=True)
    run = jnp.zeros_like(s[0:1])
    for l in range(DEPTH):
        run = run + s[l:l + 1]
        o_ref[l:l + 1, :] = run - s[0:1]


def _lower_bounds(p):
    return pl.pallas_call(
        _lower_bound_kernel,
        out_shape=jax.ShapeDtypeStruct(p.shape, F32),
        name="hgrn_lower_bounds",
    )(p)


def _gdn_kernel(q_ref, k_ref, v_ref, z_ref, ab_ref, hist_ref, s0_ref, cw_ref, hp_ref, nw_ref,
                buf_ref, o_ref, sout_ref, prev_ref, s_ref, *, chunk):
    del buf_ref
    c = pl.program_id(1)
    C = chunk

    @pl.when(c == 0)
    def _():
        prev_ref[...] = jnp.zeros_like(prev_ref)
        prev_ref[:, :, C - SUBLANES:, :] = hist_ref[0]
        s_ref[...] = s0_ref[0]

    row = lax.broadcasted_iota(jnp.int32, (C, LANES), 0)
    lane = lax.broadcasted_iota(jnp.int32, (C, LANES), 1)
    ii = lax.broadcasted_iota(jnp.int32, (C, C), 0)
    jj = lax.broadcasted_iota(jnp.int32, (C, C), 1)
    eye = ii == jj
    tri = jj <= ii
    strict = jj < ii
    same16 = (ii >> 4) == (jj >> 4)
    eye_f = jnp.where(eye, 1.0, 0.0).astype(F32)

    ab = ab_ref[...]
    g_all = -jnp.exp(hp_ref[0:1, :]) * _softplus(ab + hp_ref[1:2, :])
    beta_all = _sigmoid(ab)
    nw = nw_ref[...]

    def conv(x, xp, w):
        y = x * w[CONV_W - 1:CONV_W, :]
        for j in range(1, CONV_W):
            xs = jnp.where(row < j, pltpu.roll(xp, j, 0), pltpu.roll(x, j, 0))
            y = y + xs * w[CONV_W - 1 - j:CONV_W - j, :]
        return _silu(y)

    def l2n(x):
        return x * lax.rsqrt(jnp.sum(x * x, axis=-1, keepdims=True) + RMS_EPS)

    def load(h):
        raw = [r[h] for r in (q_ref, k_ref, v_ref)]
        prev = [prev_ref[p, h] for p in range(3)]
        taps = [cw_ref[p, h] for p in range(3)]
        return raw, prev, taps, z_ref[h], s_ref[h]

    def store(h, raw, o, s_new):
        for p in range(3):
            prev_ref[p, h] = raw[p]
        s_ref[h] = s_new
        o_ref[h] = o

    def compute(h, raw, prev, taps, z, s):
        q = l2n(conv(raw[0], prev[0], taps[0])) * (HEAD_DIM ** -0.5)
        k = l2n(conv(raw[1], prev[1], taps[1]))
        v = conv(raw[2], prev[2], taps[2])
        g_col = jnp.sum(jnp.where(lane == h, g_all, 0.0), axis=1, keepdims=True)
        b_col = jnp.sum(jnp.where(lane == h + N_HEAD, beta_all, 0.0), axis=1, keepdims=True)

        g_row = jnp.sum(jnp.where(eye, g_col, 0.0), axis=0, keepdims=True)
        gc_col = jnp.sum(jnp.where(tri, g_row, 0.0), axis=1, keepdims=True)
        gc_row = jnp.sum(jnp.where(eye, gc_col, 0.0), axis=0, keepdims=True)
        diff = gc_col - gc_row
        decay = jnp.where(tri, jnp.exp(jnp.where(tri, diff, 0.0)), 0.0)
        egc = jnp.exp(gc_col)
        g_last = gc_col[C - 1:C, :]

        kb = k * b_col
        rhs = jnp.concatenate([v * b_col, kb * egc], axis=1)
        k_dec = k * jnp.exp(g_last - gc_col)
        yield
        qk_kbk = _dot_nt(jnp.concatenate([q, kb], axis=0), k)
        attn = jnp.where(tri, qk_kbk[:C] * decay, 0.0)
        lower = jnp.where(strict, qk_kbk[C:] * decay, 0.0)

        ld = jnp.where(same16, lower, 0.0)
        lo = lower - ld
        yield
        p2 = _dot(ld, ld)
        yield
        p4 = _dot(p2, p2)
        a1 = _dot(eye_f - ld, eye_f + p2)
        yield
        p8 = _dot(p4, p4)
        a2 = _dot(a1, eye_f + p4)
        yield
        td = _dot(a2, eye_f + p8)
        yield
        y_nm = _dot(td, jnp.concatenate([rhs, lo], axis=1))
        y = y_nm[:, :2 * HEAD_DIM]
        nm = y_nm[:, 2 * HEAD_DIM:]
        yield
        n2 = _dot(nm, nm)
        yield
        zz = y + _dot(n2, y)
        yield
        sol = zz - _dot(nm, zz)
        u = sol[:, :HEAD_DIM]
        w = sol[:, HEAD_DIM:]
        yield
        ws_qs = _dot(jnp.concatenate([w, q * egc], axis=0), s)
        v_new = u - ws_qs[:C]
        yield
        o = ws_qs[C:] + _dot(attn, v_new)
        s_new = s * jnp.exp(g_last) + _dot_tn(k_dec, v_new)

        o = o * lax.rsqrt(jnp.mean(o * o, axis=-1, keepdims=True) + RMS_EPS) * nw
        return (o * _silu(z)).astype(o_ref.dtype), s_new

    def group(i, carry):
        heads = [i * GDN_HEADS_PER_STEP + u for u in range(GDN_HEADS_PER_STEP)]
        loaded = [load(h) for h in heads]
        results = _run_interleaved([compute(h, *vals) for h, vals in zip(heads, loaded)])
        for h, vals, (o, s_new) in zip(heads, loaded, results):
            store(h, vals[0], o, s_new)
        return carry

    lax.fori_loop(0, N_HEAD // GDN_HEADS_PER_STEP, group, 0)

    @pl.when(c == pl.num_programs(1) - 1)
    def _():
        sout_ref[0] = s_ref[...]


def _gdn(proj, ab, hist, s0, cw, hp, nw, buf, layer, *, batch, row0, t_pad, chunk):
    nc = t_pad // chunk
    rb0 = row0 // chunk
    l_state = layer if hist.shape[0] > 1 else 0

    def sec(s):
        return pl.BlockSpec((N_HEAD, chunk, LANES), lambda b, c: (s, rb0 + b * nc + c, 0))

    kern = functools.partial(_gdn_kernel, chunk=chunk)
    return pl.pallas_call(
        kern,
        grid=(batch, nc),
        in_specs=[sec(SEC_GQ), sec(SEC_GK), sec(SEC_GV), sec(SEC_GZ),
                  pl.BlockSpec((chunk, LANES), lambda b, c: (rb0 + b * nc + c, 0)),
                  pl.BlockSpec((None, 1, 3, N_HEAD, SUBLANES, LANES),
                               lambda b, c: (l_state, b, 0, 0, 0, 0)),
                  pl.BlockSpec((None, 1, N_HEAD, HEAD_DIM, HEAD_DIM),
                               lambda b, c: (l_state, b, 0, 0, 0)),
                  pl.BlockSpec((None, 3, N_HEAD, SUBLANES, LANES), lambda b, c: (layer, 0, 0, 0, 0)),
                  pl.BlockSpec((None, SUBLANES, LANES), lambda b, c: (layer, 0, 0)),
                  pl.BlockSpec((None, 1, LANES), lambda b, c: (layer, 0, 0)),
                  pl.BlockSpec(memory_space=pl.ANY)],
        out_specs=[pl.BlockSpec((None, N_HEAD, chunk, LANES),
                                lambda b, c: (BR_GDN, 0, rb0 + b * nc + c, 0)),
                   pl.BlockSpec((1, N_HEAD, HEAD_DIM, HEAD_DIM), lambda b, c: (b, 0, 0, 0))],
        out_shape=[jax.ShapeDtypeStruct(buf.shape, buf.dtype),
                   jax.ShapeDtypeStruct((batch, N_HEAD, HEAD_DIM, HEAD_DIM), F32)],
        input_output_aliases={10: 0},
        scratch_shapes=[pltpu.VMEM((3, N_HEAD, chunk, LANES), F32),
                        pltpu.VMEM((N_HEAD, HEAD_DIM, HEAD_DIM), F32)],
        compiler_params=pltpu.CompilerParams(
            dimension_semantics=("arbitrary", "arbitrary"),
            vmem_limit_bytes=32 * 1024 * 1024),
        name="gdn_mixer",
    )(proj, proj, proj, proj, ab, hist, s0, cw, hp, nw, buf)


def _hgrn_kernel(q_ref, f_ref, i_ref, g_ref, lb_ref, s0_ref, nw_ref, buf_ref, o_ref, sout_ref,
                 s_ref, *, chunk):
    del buf_ref
    c = pl.program_id(1)
    C = chunk
    halves = [r for r in (32, 16, 8) if 2 * r <= C]

    @pl.when(c == 0)
    def _():
        s_ref[...] = s0_ref[0]

    ii = lax.broadcasted_iota(jnp.int32, (C, C), 0)
    jj = lax.broadcasted_iota(jnp.int32, (C, C), 1)
    sel_blocks = [jnp.where(jj <= ii, 1.0, 0.0)]
    level_masks = []
    for r in halves:
        ref_row = (ii & ~(2 * r - 1)) + (r - 1)
        sel_blocks.append(jnp.where(jj <= ref_row, 1.0, 0.0))
        same_pair = (ii & ~(2 * r - 1)) == (jj & ~(2 * r - 1))
        level_masks.append(same_pair & ((ii & r) != 0) & ((jj & r) == 0))
    sel = jnp.concatenate(sel_blocks, axis=0).astype(BF16)
    nw = nw_ref[...]
    row8 = lax.broadcasted_iota(jnp.int32, (SUBLANES, LANES), 0)

    def load(h):
        return q_ref[h], f_ref[h], i_ref[h], g_ref[h], lb_ref[h], s_ref[h]

    def compute(q_raw, f_raw, v, g_raw, lb, st):
        q = _silu(q_raw)
        f = lb + (1.0 - lb) * _sigmoid(f_raw)
        logf = jnp.log(f)
        k = 1.0 - f
        yield
        cums = _dot_sel_lhs(sel, logf)
        b = cums[:C]
        q_dec = q * jnp.exp(b)
        b_last = b[C - 1:C]
        k_dec = k * jnp.exp(b_last - b)
        yield
        attn = jnp.zeros((C, C), F32)
        for l, mask in enumerate(level_masks):
            bref = cums[(l + 1) * C:(l + 2) * C]
            qt = q * jnp.exp(jnp.minimum(b - bref, 0.0))
            kt = k * jnp.exp(jnp.minimum(bref - b, 0.0))
            attn = attn + jnp.where(mask, _dot_nt(qt, kt), 0.0)
        o_inter = _dot_nt(q_dec, st)
        st_new = st * jnp.exp(b_last) + _dot_tn(v, k_dec)
        yield
        o = o_inter + _dot(attn, v)

        diag = []
        for d in range(C // SUBLANES):
            sl = slice(d * SUBLANES, (d + 1) * SUBLANES)
            qd, kd, bd, vd = q[sl], k[sl], b[sl], v[sl]
            od = jnp.zeros((SUBLANES, LANES), F32)
            for s in range(SUBLANES):
                e = jnp.exp(jnp.minimum(bd - bd[s:s + 1], 0.0))
                wgt = jnp.sum(qd * kd[s:s + 1] * e, axis=1, keepdims=True)
                od = od + jnp.where(row8 >= s, wgt, 0.0) * vd[s:s + 1]
            diag.append(od)
        o = o + jnp.concatenate(diag, axis=0)

        o = o * lax.rsqrt(jnp.mean(o * o, axis=-1, keepdims=True) + RMS_EPS) * nw
        return (o * _silu(g_raw)).astype(o_ref.dtype), st_new

    def group(i, carry):
        heads = [i * HGRN_HEADS_PER_STEP + u for u in range(HGRN_HEADS_PER_STEP)]
        results = _run_interleaved([compute(*vals) for vals in [load(h) for h in heads]])
        for h, (o, st_new) in zip(heads, results):
            s_ref[h] = st_new
            o_ref[h] = o
        return carry

    lax.fori_loop(0, N_HEAD // HGRN_HEADS_PER_STEP, group, 0)

    @pl.when(c == pl.num_programs(1) - 1)
    def _():
        sout_ref[0] = s_ref[...]


def _hgrn(proj, lb, s0_t, nw, buf, layer, *, batch, row0, t_pad, chunk):
    nc = t_pad // chunk
    rb0 = row0 // chunk
    l_state = layer if s0_t.shape[0] > 1 else 0

    def sec(s):
        return pl.BlockSpec((N_HEAD, chunk, LANES), lambda b, c: (s, rb0 + b * nc + c, 0))

    kern = functools.partial(_hgrn_kernel, chunk=chunk)
    return pl.pallas_call(
        kern,
        grid=(batch, nc),
        in_specs=[sec(SEC_HQ), sec(SEC_HF), sec(SEC_HI), sec(SEC_HG),
                  pl.BlockSpec((None, N_HEAD, 1, LANES), lambda b, c: (layer, 0, 0, 0)),
                  pl.BlockSpec((None, 1, N_HEAD, HEAD_DIM, HEAD_DIM),
                               lambda b, c: (l_state, b, 0, 0, 0)),
                  pl.BlockSpec((None, 1, LANES), lambda b, c: (layer, 0, 0)),
                  pl.BlockSpec(memory_space=pl.ANY)],
        out_specs=[pl.BlockSpec((None, N_HEAD, chunk, LANES),
                                lambda b, c: (BR_HGRN, 0, rb0 + b * nc + c, 0)),
                   pl.BlockSpec((1, N_HEAD, HEAD_DIM, HEAD_DIM), lambda b, c: (b, 0, 0, 0))],
        out_shape=[jax.ShapeDtypeStruct(buf.shape, buf.dtype),
                   jax.ShapeDtypeStruct((batch, N_HEAD, HEAD_DIM, HEAD_DIM), F32)],
        input_output_aliases={7: 0},
        scratch_shapes=[pltpu.VMEM((N_HEAD, HEAD_DIM, HEAD_DIM), F32)],
        compiler_params=pltpu.CompilerParams(
            dimension_semantics=("arbitrary", "arbitrary"),
            vmem_limit_bytes=32 * 1024 * 1024),
        name="hgrn_mixer",
    )(proj, proj, proj, proj, lb, s0_t, nw, buf)


def _sb_kernel(*refs, q_block, t_new, t_past, pad_front):
    if t_past:
        q_ref, kn_ref, vn_ref, kp_ref, vp_ref, _, o_ref, ko_ref, vo_ref, kb_ref, vb_ref = refs
    else:
        q_ref, kn_ref, vn_ref, _, o_ref, ko_ref, vo_ref, kb_ref, vb_ref = refs
    qi = pl.program_id(2)
    t_all = t_past + t_new
    t_scr = kb_ref.shape[0]
    n_sub = SB_KEY_SLAB // LANES

    @pl.when(qi == 0)
    def _():
        k_new = kn_ref[0]
        v_new = vn_ref[0]
        ko_ref[0] = k_new[pad_front:, :]
        vo_ref[0] = v_new[pad_front:, :]
        if t_past:
            head_rows = pl.ds(pl.program_id(1), t_past, stride=N_HEAD)
            kb_ref[0:t_past, :] = kp_ref[0, head_rows, :].astype(BF16)
            vb_ref[0:t_past, :] = vp_ref[0, head_rows, :].astype(BF16)
        kb_ref[t_past:t_all, :] = k_new.astype(BF16)
        vb_ref[t_past:t_all, :] = v_new.astype(BF16)
        if t_scr > t_all:
            kb_ref[t_all:, :] = jnp.zeros((t_scr - t_all, LANES), BF16)
            vb_ref[t_all:, :] = jnp.zeros((t_scr - t_all, LANES), BF16)

    n_part = 2 if q_block % (4 * SUBLANES * 2) == 0 and q_block >= LANES else 1
    rows = q_block // n_part

    q = q_ref[0].astype(BF16)
    q_pos = t_past + qi * q_block + lax.broadcasted_iota(jnp.int32, (rows, SB_KEY_SLAB), 0)
    lane = lax.broadcasted_iota(jnp.int32, (rows, SB_KEY_SLAB), 1)
    jr = lax.broadcasted_iota(jnp.int32, (LANES, 2 * LANES), 0)
    sc = lax.broadcasted_iota(jnp.int32, (LANES, 2 * LANES), 1)
    suffix_and_total = jnp.where((sc >= LANES) | (jr > sc), 1.0, 0.0).astype(BF16)
    slab_hi = (t_past + qi * q_block + q_block - 1) // SB_KEY_SLAB
    scale = HEAD_DIM ** -0.5

    def cond(carry):
        it, _, _, active = carry
        return jnp.logical_and(it <= slab_hi, active > 0)

    def part(p, slab, kt, vt, run, acc):
        sl = slice(p * rows, (p + 1) * rows)
        z = lax.dot_general(q[sl], kt, (((1,), (1,)), ((), ())), preferred_element_type=F32) * scale
        yield
        k_pos = slab * SB_KEY_SLAB + lane
        vis = (k_pos < q_pos + p * rows) & (k_pos >= pad_front)
        sp = _softplus(z)
        log_keep = jnp.where(vis, -sp, 0.0)
        between = [None] * n_sub
        for j in reversed(range(n_sub)):
            sums = _dot_sel_rhs(log_keep[:, j * LANES:(j + 1) * LANES], suffix_and_total)
            between[j] = sums[:, :LANES] + run
            run = run + sums[:, LANES:]
        yield
        a = jnp.where(vis, jnp.exp((z - sp) + jnp.concatenate(between, axis=1)), 0.0)
        acc = acc + jnp.dot(a.astype(BF16), vt, preferred_element_type=F32)
        return run, acc

    def body(carry):
        it, run, acc, _ = carry
        slab = slab_hi - it
        start = pl.multiple_of(slab * SB_KEY_SLAB, SB_KEY_SLAB)
        kt = kb_ref[pl.ds(start, SB_KEY_SLAB), :]
        vt = vb_ref[pl.ds(start, SB_KEY_SLAB), :]
        done = _run_interleaved([
            part(p, slab, kt, vt, run[p * rows:(p + 1) * rows], acc[p * rows:(p + 1) * rows])
            for p in range(n_part)])
        run = jnp.concatenate([r for r, _ in done], axis=0)
        acc = jnp.concatenate([a for _, a in done], axis=0)
        return it + 1, run, acc, (jnp.max(run) > EXP_UNDERFLOW).astype(jnp.int32)

    zero = jnp.zeros((q_block, LANES), F32)
    _, _, acc, _ = lax.while_loop(cond, body, (jnp.int32(0), zero, zero, jnp.int32(1)))
    o_ref[...] = acc.astype(o_ref.dtype)


def _sb(proj, past_k, past_v, buf, layer, *, batch, row0, t_pad, q_block, t_past, pad_front):
    nq = t_pad // q_block
    rbq0 = row0 // q_block
    rbk0 = row0 // t_pad
    t_scr = -(-(t_past + t_pad) // SB_KEY_SLAB) * SB_KEY_SLAB
    t_out = t_pad - pad_front
    sec_q = SEC_SQ * N_HEAD
    sec_k = SEC_SK * N_HEAD
    sec_v = SEC_SV * N_HEAD
    in_specs = [pl.BlockSpec((1, q_block, LANES), lambda b, h, i: (sec_q + h, rbq0 + b * nq + i, 0)),
                pl.BlockSpec((1, t_pad, LANES), lambda b, h, i: (sec_k + h, rbk0 + b, 0)),
                pl.BlockSpec((1, t_pad, LANES), lambda b, h, i: (sec_v + h, rbk0 + b, 0))]
    args = [proj, proj, proj]
    if t_past:
        past_spec = pl.BlockSpec((None, 1, t_past * N_HEAD, LANES), lambda b, h, i: (layer, b, 0, 0))
        in_specs += [past_spec, past_spec]
        args += [past_k, past_v]
    in_specs.append(pl.BlockSpec(memory_space=pl.ANY))
    args.append(buf)
    kv_out = pl.BlockSpec((1, t_out, LANES), lambda b, h, i: (b, 0, h))
    kern = functools.partial(_sb_kernel, q_block=q_block, t_new=t_pad, t_past=t_past,
                             pad_front=pad_front)
    return pl.pallas_call(
        kern,
        grid=(batch, N_HEAD, nq),
        in_specs=in_specs,
        out_specs=[pl.BlockSpec((None, None, q_block, LANES),
                                lambda b, h, i: (BR_SB, h, rbq0 + b * nq + i, 0)),
                   kv_out, kv_out],
        input_output_aliases={len(args) - 1: 0},
        out_shape=[jax.ShapeDtypeStruct(buf.shape, buf.dtype),
                   jax.ShapeDtypeStruct((batch, t_out, MIX_WIDTH), F32),
                   jax.ShapeDtypeStruct((batch, t_out, MIX_WIDTH), F32)],
        scratch_shapes=[pltpu.VMEM((t_scr, LANES), BF16), pltpu.VMEM((t_scr, LANES), BF16)],
        compiler_params=pltpu.CompilerParams(
            dimension_semantics=("arbitrary", "arbitrary", "arbitrary"),
            vmem_limit_bytes=_vmem_limit(2 * t_past * N_HEAD * LANES * 4, 4 * t_pad * LANES * 4)),
        name="stickbreak_attention",
    )(*args)


def _last_rows(proj_a, row_lo, row_hi, t_seq):
    n_qkv = 3 * N_HEAD
    rows = [lax.slice(proj_a, (0, row_lo + t_seq - (CONV_W - 1) + r, 0),
                      (n_qkv, row_hi, HEAD_DIM), (1, t_seq, 1)) for r in range(CONV_W - 1)]
    taps = jnp.stack(rows)
    return taps.transpose(2, 0, 1, 3).reshape(taps.shape[2], CONV_W - 1, 3 * MIX_WIDTH)


def kernel(x_prompt, x_sample, cache_sb_k, cache_sb_v, state_gdn_conv, state_gdn, state_hgrn,
           meta_tokens, norm1_w, w_in, gdn_conv_w, gdn_a_log, gdn_dt_bias, gdn_norm_w,
           hgrn_lb_param, hgrn_norm_w, w_branch, w_o, norm2_w, w_ff_gate, w_ff_up, w_ff_down,
           final_norm_w):
    b_p, seq, _ = x_prompt.shape
    b_s, t_s, _ = x_sample.shape
    t_p = N_META + seq
    sb_qb = SB_Q_BLOCK_P if t_p > SB_Q_BLOCK_P else CHUNK_P
    t_pp = -(-t_p // sb_qb) * sb_qb
    pad_front = t_pp - t_p
    m_p = b_p * t_pp
    m_s = b_s * t_s
    m_all = m_p + m_s
    past_len = cache_sb_k.shape[2]
    tm = _row_tile(m_all, 1088)
    t_skip = pad_front + N_META
    assert t_pp % CHUNK_P == 0 and pad_front % SUBLANES == 0 and t_skip % CHUNK_P == 0

    meta = jnp.broadcast_to(meta_tokens[None], (b_p, N_META, D_MODEL)).astype(F32)
    xp = jnp.concatenate([jnp.zeros((b_p, pad_front, D_MODEL), F32), meta, x_prompt], axis=1)
    x = jnp.concatenate([xp.reshape(m_p, D_MODEL), x_sample.reshape(m_s, D_MODEL)], axis=0)

    n_bg = w_in.shape[2] - _W_IN_SB
    w_in_a = _cast_bf16(w_in, block=(128, _W_IN_AB), out_cols=_W_IN_AB)
    w_in_bg = _cast_bf16_shifted(w_in, col0=_W_IN_SB, n_cols=n_bg, block=(512, 2048))
    w_in_ab = jnp.pad(w_in[:, :, _W_IN_AB:_W_IN_SB],
                      ((0, 0), (0, 0), (0, LANES - 2 * N_HEAD))).astype(BF16)
    w_br = _cast_bf16(w_branch.reshape(DEPTH * N_BRANCH, MIX_WIDTH, D_MODEL), block=(256, D_MODEL))
    w_out = _cast_bf16(w_o, block=(256, D_MODEL))
    w_fg = _cast_bf16(w_ff_gate, block=(D_MODEL, 256), out_cols=D_FF_PAD)
    w_fu = _cast_bf16(w_ff_up, block=(D_MODEL, 256), out_cols=D_FF_PAD)
    w_fd = _cast_bf16(w_ff_down, block=(256, D_MODEL), out_rows=D_FF_PAD)

    lower_bounds = _lower_bounds(hgrn_lb_param).reshape(DEPTH, N_HEAD, 1, HEAD_DIM)
    cw = gdn_conv_w.reshape(DEPTH, CONV_W, 3, N_HEAD, HEAD_DIM).transpose(0, 2, 3, 1, 4)
    cw = jnp.pad(cw, ((0, 0), (0, 0), (0, 0), (0, SUBLANES - CONV_W), (0, 0)))
    hp = jnp.zeros((DEPTH, SUBLANES, LANES), F32)
    hp = hp.at[:, 0, :N_HEAD].set(gdn_a_log).at[:, 1, :N_HEAD].set(gdn_dt_bias)
    gnw = gdn_norm_w.reshape(DEPTH, 1, HEAD_DIM)
    hnw = hgrn_norm_w.reshape(DEPTH, 1, HEAD_DIM)
    hist_s = state_gdn_conv.reshape(DEPTH, b_s, CONV_W - 1, 3, N_HEAD, HEAD_DIM)
    hist_s = jnp.pad(hist_s.transpose(0, 1, 3, 4, 2, 5),
                     ((0, 0),) * 4 + ((SUBLANES - (CONV_W - 1), 0), (0, 0)))
    zero_hist = jnp.zeros((1, b_p, 3, N_HEAD, SUBLANES, LANES), F32)
    zero_state = jnp.zeros((1, b_p, N_HEAD, HEAD_DIM, HEAD_DIM), F32)
    hgrn_s0 = state_hgrn.swapaxes(-1, -2)
    past_k = cache_sb_k.reshape(DEPTH, b_s, past_len * N_HEAD, HEAD_DIM)
    past_v = cache_sb_v.reshape(DEPTH, b_s, past_len * N_HEAD, HEAD_DIM)
    branches = jnp.zeros((N_BRANCH, N_HEAD, m_all, LANES), BF16)

    outs = {name: [] for name in ("kp", "vp", "ks", "vs", "cp", "cs", "gp", "gs", "hp", "hs")}
    for l in range(DEPTH):
        h = _rmsnorm(x, norm1_w[l], BF16)
        proj_a = _dense(h, w_in_a, l, tm=tm, tn=1024, mode="headmajor", name="proj_gdn")
        proj_b = _dense(h, w_in_bg, l, tm=tm, tn=1024, mode="headmajor", n=7 * MIX_WIDTH,
                        name="proj_sb_hgrn")
        ab = _dense(h, w_in_ab, l, tm=tm, tn=LANES, name="proj_ab")
        gates = _dense(h, w_in_bg, l, tm=tm, tn=1024, mode="sigmoid", out_dtype=BF16,
                       col0=7 * MIX_WIDTH, name="proj_gates")

        branches, gs_p = _gdn(proj_a, ab, zero_hist, zero_state, cw, hp, gnw, branches, l,
                              batch=b_p, row0=0, t_pad=t_pp, chunk=CHUNK_P)
        branches, gs_s = _gdn(proj_a, ab, hist_s, state_gdn, cw, hp, gnw, branches, l,
                              batch=b_s, row0=m_p, t_pad=t_s, chunk=t_s)
        branches, k_p, v_p = _sb(proj_b, None, None, branches, l, batch=b_p, row0=0, t_pad=t_pp,
                                 q_block=sb_qb, t_past=0, pad_front=pad_front)
        branches, k_s, v_s = _sb(proj_b, past_k, past_v, branches, l, batch=b_s, row0=m_p,
                                 t_pad=t_s, q_block=t_s, t_past=past_len, pad_front=0)
        branches, hs_p = _hgrn(proj_b, lower_bounds, zero_state, hnw, branches, l,
                               batch=b_p, row0=0, t_pad=t_pp, chunk=CHUNK_P)
        branches, hs_s = _hgrn(proj_b, lower_bounds, hgrn_s0, hnw, branches, l,
                               batch=b_s, row0=m_p, t_pad=t_s, chunk=t_s)

        merged = _merge(branches, w_br, l, gates, tm=tm, tn=1024)
        x = _dense(merged, w_out, l, tm=tm, tn=1024, mode="residual", residual=x, name="out_proj")

        h2 = _rmsnorm(x, norm2_w[l], BF16)
        hmid = _dense(h2, w_fg, l, tm=tm, tn=512, mode="swiglu", out_dtype=BF16, w2=w_fu,
                      name="ffn_gate_up")
        x = _dense_ktiled_residual(hmid, w_fd, l, x, tm=tm, tn=1024, tk=1408, name="ffn_down")

        outs["kp"].append(k_p.reshape(b_p, t_p, N_HEAD, HEAD_DIM))
        outs["vp"].append(v_p.reshape(b_p, t_p, N_HEAD, HEAD_DIM))
        outs["ks"].append(k_s.reshape(b_s, t_s, N_HEAD, HEAD_DIM))
        outs["vs"].append(v_s.reshape(b_s, t_s, N_HEAD, HEAD_DIM))
        outs["cp"].append(_last_rows(proj_a, 0, m_p, t_pp))
        outs["cs"].append(_last_rows(proj_a, m_p, m_all, t_s))
        outs["gp"].append(gs_p)
        outs["gs"].append(gs_s)
        outs["hp"].append(hs_p.swapaxes(-1, -2))
        outs["hs"].append(hs_s.swapaxes(-1, -2))

    tf = CHUNK_P
    y_prompt = _rmsnorm(
        x, final_norm_w, F32, tm=tf, grid=(b_p, seq // tf),
        in_map=lambda b, i: (b * (t_pp // tf) + t_skip // tf + i, 0),
        out_map=lambda b, i: (b * (seq // tf) + i, 0), out_rows=b_p * seq).reshape(b_p, seq, D_MODEL)
    ts = _row_tile(m_s, 256)
    y_sample = _rmsnorm(
        x, final_norm_w, F32, tm=ts, grid=(m_s // ts,),
        in_map=lambda i: (m_p // ts + i, 0), out_map=lambda i: (i, 0),
        out_rows=m_s).reshape(b_s, t_s, D_MODEL)
    st = {k: jnp.stack(v) for k, v in outs.items()}
    return (y_prompt, y_sample, st["kp"], st["vp"], st["ks"], st["vs"], st["cp"], st["cs"],
            st["gp"], st["gs"], st["hp"], st["hs"])
```

```python
import functools

import jax
import jax.numpy as jnp
from jax import lax
from jax.experimental import pallas as pl
from jax.experimental.pallas import tpu as pltpu

F32 = jnp.float32
BF16 = jnp.bfloat16

D_MODEL = 4096
DEPTH = 4
N_HEAD = 16
HEAD_DIM = 128
MIX_WIDTH = N_HEAD * HEAD_DIM
N_META = 16
CONV_W = 4
D_FF = 11008
RMS_EPS = 1e-6
N_BRANCH = 3

LANES = 128
SUBLANES = 8
VMEM_LIMIT_CAP = 60 * 1024 * 1024

CHUNK_P = 64
SB_Q_BLOCK_P = 192
SB_KEY_SLAB = 3 * LANES
D_FF_PAD = -(-D_FF // 1024) * 1024
GDN_HEADS_PER_STEP = 16
HGRN_HEADS_PER_STEP = 8
EXP_UNDERFLOW = -104.0
SEC_GQ, SEC_GK, SEC_GV, SEC_GZ = range(4)
SEC_SQ, SEC_SK, SEC_SV, SEC_HQ, SEC_HF, SEC_HI, SEC_HG = range(7)
BR_GDN, BR_SB, BR_HGRN = range(3)
_W_IN_AB = 4 * MIX_WIDTH
_W_IN_SB = _W_IN_AB + 2 * N_HEAD
_W_IN_GATE = _W_IN_SB + 7 * MIX_WIDTH


def _vmem_limit(*block_bytes):
    need = 2 * sum(block_bytes) + 2 * max(block_bytes)
    return int(min(max(need, 16 * 1024 * 1024), VMEM_LIMIT_CAP))


def _sigmoid(x):
    return 1.0 / (1.0 + jnp.exp(-x))


def _silu(x):
    return x * _sigmoid(x)


def _softplus(x):
    return jnp.maximum(x, 0.0) + jnp.log1p(jnp.exp(-jnp.abs(x)))


def _dot(a, b):
    return jnp.dot(a.astype(BF16), b.astype(BF16), preferred_element_type=F32)


def _dot_nt(a, b):
    return lax.dot_general(a.astype(BF16), b.astype(BF16), (((1,), (1,)), ((), ())),
                           preferred_element_type=F32)


def _dot_tn(a, b):
    return lax.dot_general(a.astype(BF16), b.astype(BF16), (((0,), (0,)), ((), ())),
                           preferred_element_type=F32)


def _split(x):
    hi = x.astype(BF16)
    lo = (x - hi.astype(F32)).astype(BF16)
    return hi, lo


def _dot_sel_lhs(sel, x):
    hi, lo = _split(x)
    d = functools.partial(jnp.dot, preferred_element_type=F32)
    return d(sel, hi) + d(sel, lo)


def _dot_sel_rhs(x, sel):
    hi, lo = _split(x)
    d = functools.partial(jnp.dot, preferred_element_type=F32)
    return d(hi, sel) + d(lo, sel)


def _run_interleaved(gens):
    results = [None] * len(gens)
    live = list(range(len(gens)))
    while live:
        for i in list(live):
            try:
                next(gens[i])
            except StopIteration as done:
                results[i] = done.value
                live.remove(i)
    return results


def _row_tile(m, cap):
    return max(t for t in range(16, cap + 1, 16) if m % t == 0)


def _cast_kernel(x_ref, o_ref, *, pad_axis, n_real):
    if pad_axis is None:
        o_ref[...] = x_ref[...].astype(o_ref.dtype)
        return
    i = pl.program_id(pad_axis)

    @pl.when(i < n_real)
    def _():
        o_ref[...] = x_ref[...].astype(o_ref.dtype)

    @pl.when(i >= n_real)
    def _():
        o_ref[...] = jnp.zeros_like(o_ref)


def _cast_bf16(w, *, block, out_rows=None, out_cols=None):
    n_l, n_r, n_c = w.shape
    br, bc = block
    out_rows = n_r if out_rows is None else out_rows
    out_cols = n_c if out_cols is None else out_cols
    pad_axis, n_real = None, 0
    if out_rows > n_r:
        assert n_r % br == 0 and out_cols <= n_c
        pad_axis, n_real = 1, n_r // br
    elif out_cols > n_c:
        assert n_c % bc == 0
        pad_axis, n_real = 2, n_c // bc
    last_r = -(-n_r // br) - 1
    last_c = -(-n_c // bc) - 1
    kern = functools.partial(_cast_kernel, pad_axis=pad_axis, n_real=n_real)
    return pl.pallas_call(
        kern,
        grid=(n_l, out_rows // br, out_cols // bc),
        in_specs=[pl.BlockSpec((None, br, bc),
                               lambda l, i, j: (l, jnp.minimum(i, last_r), jnp.minimum(j, last_c)))],
        out_specs=pl.BlockSpec((None, br, bc), lambda l, i, j: (l, i, j)),
        out_shape=jax.ShapeDtypeStruct((n_l, out_rows, out_cols), BF16),
        compiler_params=pltpu.CompilerParams(
            dimension_semantics=("arbitrary",) * 3,
            vmem_limit_bytes=_vmem_limit(br * bc * 4, br * bc * 2)),
        name="cast_bf16",
    )(w)


def _rmsnorm_kernel(x_ref, w_ref, o_ref):
    x = x_ref[...]
    ms = jnp.mean(x * x, axis=-1, keepdims=True)
    o_ref[...] = (x * lax.rsqrt(ms + RMS_EPS) * w_ref[...]).astype(o_ref.dtype)


def _rmsnorm(x, w, out_dtype, *, tm=None, grid=None, in_map=None, out_map=None, out_rows=None):
    m, d = x.shape
    if tm is None:
        tm = _row_tile(m, 256)
    if grid is None:
        grid, in_map, out_map, out_rows = (m // tm,), (lambda i: (i, 0)), (lambda i: (i, 0)), m
    return pl.pallas_call(
        _rmsnorm_kernel,
        grid=grid,
        in_specs=[pl.BlockSpec((tm, d), in_map),
                  pl.BlockSpec((1, d), lambda *g: (0, 0))],
        out_specs=pl.BlockSpec((tm, d), out_map),
        out_shape=jax.ShapeDtypeStruct((out_rows, d), out_dtype),
        compiler_params=pltpu.CompilerParams(
            dimension_semantics=("arbitrary",) * len(grid),
            vmem_limit_bytes=_vmem_limit(tm * d * 4, tm * d * 4)),
        name="rmsnorm",
    )(x, w.reshape(1, d))


def _mm_kernel(a_ref, w_ref, o_ref):
    o_ref[...] = jnp.dot(a_ref[...], w_ref[...], preferred_element_type=F32).astype(o_ref.dtype)


def _mm_headmajor_kernel(a_ref, w_ref, o_ref):
    y = jnp.dot(a_ref[...], w_ref[...], preferred_element_type=F32)
    for j in range(o_ref.shape[0]):
        o_ref[j] = y[:, j * LANES:(j + 1) * LANES].astype(o_ref.dtype)


def _mm_sigmoid_kernel(a_ref, w_ref, o_ref):
    y = jnp.dot(a_ref[...], w_ref[...], preferred_element_type=F32)
    o_ref[...] = _sigmoid(y).astype(o_ref.dtype)


def _mm_residual_kernel(a_ref, w_ref, r_ref, o_ref):
    y = jnp.dot(a_ref[...], w_ref[...], preferred_element_type=F32)
    o_ref[...] = r_ref[...] + y


def _mm_swiglu_kernel(a_ref, wg_ref, wu_ref, o_ref):
    a = a_ref[...]
    g = jnp.dot(a, wg_ref[...], preferred_element_type=F32)
    u = jnp.dot(a, wu_ref[...], preferred_element_type=F32)
    o_ref[...] = (_silu(g) * u).astype(o_ref.dtype)


def _dense(a, w, layer, *, tm, tn, mode="plain", out_dtype=F32, residual=None, w2=None,
           col0=0, n=None, name="dense"):
    m, k = a.shape
    n = w.shape[2] - col0 if n is None else n
    assert col0 % tn == 0 and n % tn == 0
    grid = (m // tm, n // tn)
    a_spec = pl.BlockSpec((tm, k), lambda i, j: (i, 0))
    w_spec = pl.BlockSpec((None, k, tn), lambda i, j: (layer, 0, col0 // tn + j))
    o_spec = pl.BlockSpec((tm, tn), lambda i, j: (i, j))
    out_shape = jax.ShapeDtypeStruct((m, n), out_dtype)
    out_bytes = tm * tn * jnp.dtype(out_dtype).itemsize
    blocks = [tm * k * 2, k * tn * 2, out_bytes, tm * tn * 4]
    if mode == "plain":
        kern, ins, specs = _mm_kernel, (a, w), [a_spec, w_spec]
    elif mode == "headmajor":
        kern, ins, specs = _mm_headmajor_kernel, (a, w), [a_spec, w_spec]
        o_spec = pl.BlockSpec((tn // LANES, tm, LANES), lambda i, j: (j, i, 0))
        out_shape = jax.ShapeDtypeStruct((n // LANES, m, LANES), out_dtype)
    elif mode == "sigmoid":
        kern, ins, specs = _mm_sigmoid_kernel, (a, w), [a_spec, w_spec]
    elif mode == "residual":
        kern, ins, specs = _mm_residual_kernel, (a, w, residual), [a_spec, w_spec, o_spec]
        blocks.append(tm * tn * 4)
    elif mode == "swiglu":
        kern, ins, specs = _mm_swiglu_kernel, (a, w, w2), [a_spec, w_spec, w_spec]
        blocks += [k * tn * 2, tm * tn * 4]
    else:
        raise ValueError(mode)
    return pl.pallas_call(
        kern, grid=grid, in_specs=specs, out_specs=o_spec, out_shape=out_shape,
        compiler_params=pltpu.CompilerParams(
            dimension_semantics=("arbitrary", "arbitrary"),
            vmem_limit_bytes=_vmem_limit(*blocks)),
        name=name,
    )(*ins)


def _mm_ktiled_kernel(a_ref, w_ref, r_ref, o_ref, acc_ref):
    kk = pl.program_id(2)
    y = jnp.dot(a_ref[...], w_ref[...], preferred_element_type=F32)

    @pl.when(kk == 0)
    def _():
        acc_ref[...] = r_ref[...] + y

    @pl.when(kk > 0)
    def _():
        acc_ref[...] += y

    @pl.when(kk == pl.num_programs(2) - 1)
    def _():
        o_ref[...] = acc_ref[...]


def _dense_ktiled_residual(a, w, layer, residual, *, tm, tn, tk, name):
    m, k = a.shape
    n = w.shape[2]
    return pl.pallas_call(
        _mm_ktiled_kernel,
        grid=(m // tm, n // tn, k // tk),
        in_specs=[pl.BlockSpec((tm, tk), lambda i, j, kk: (i, kk)),
                  pl.BlockSpec((None, tk, tn), lambda i, j, kk: (layer, kk, j)),
                  pl.BlockSpec((tm, tn), lambda i, j, kk: (i, j))],
        out_specs=pl.BlockSpec((tm, tn), lambda i, j, kk: (i, j)),
        out_shape=jax.ShapeDtypeStruct((m, n), F32),
        scratch_shapes=[pltpu.VMEM((tm, tn), F32)],
        compiler_params=pltpu.CompilerParams(
            dimension_semantics=("arbitrary", "arbitrary", "arbitrary"),
            vmem_limit_bytes=_vmem_limit(tm * tk * 2, tk * tn * 2, tm * tn * 4, tm * tn * 4,
                                         tm * tn * 4)),
        name=name,
    )(a, w, residual)


def _merge_kernel(o_ref, w_ref, g_ref, out_ref, acc_ref):
    br = pl.program_id(2)
    a = jnp.concatenate([o_ref[h] for h in range(N_HEAD)], axis=1)
    y = jnp.dot(a, w_ref[...], preferred_element_type=F32)
    contrib = g_ref[...].astype(F32) * y

    @pl.when(br == 0)
    def _():
        acc_ref[...] = contrib

    @pl.when(br > 0)
    def _():
        acc_ref[...] += contrib

    @pl.when(br == N_BRANCH - 1)
    def _():
        out_ref[...] = acc_ref[...].astype(out_ref.dtype)


def _merge(branches, w_br, layer, gates, *, tm, tn):
    m = branches.shape[2]
    n_col = D_MODEL // tn
    return pl.pallas_call(
        _merge_kernel,
        grid=(m // tm, n_col, N_BRANCH),
        in_specs=[pl.BlockSpec((None, N_HEAD, tm, LANES), lambda i, j, g: (g, 0, i, 0)),
                  pl.BlockSpec((None, MIX_WIDTH, tn), lambda i, j, g: (layer * N_BRANCH + g, 0, j)),
                  pl.BlockSpec((tm, tn), lambda i, j, g: (i, g * n_col + j))],
        out_specs=pl.BlockSpec((tm, tn), lambda i, j, g: (i, j)),
        out_shape=jax.ShapeDtypeStruct((m, D_MODEL), BF16),
        scratch_shapes=[pltpu.VMEM((tm, tn), F32)],
        compiler_params=pltpu.CompilerParams(
            dimension_semantics=("arbitrary", "arbitrary", "arbitrary"),
            vmem_limit_bytes=_vmem_limit(tm * MIX_WIDTH * 2, MIX_WIDTH * tn * 2, tm * tn * 2,
                                         tm * tn * 2, tm * tn * 4, tm * MIX_WIDTH * 2)),
        name="branch_merge",
    )(branches, w_br, gates)


def _lower_bound_kernel(p_ref, o_ref):
    p = p_ref[...]
    e = jnp.exp(p - jnp.max(p, axis=0, keepdims=True))
    s = e / jnp.sum(e, axis=0, keepdims=True)
    run = jnp.zeros_like(s[0:1])
    for l in range(DEPTH):
        run = run + s[l:l + 1]
        o_ref[l:l + 1, :] = run - s[0:1]


def _lower_bounds(p):
    return pl.pallas_call(
        _lower_bound_kernel,
        out_shape=jax.ShapeDtypeStruct(p.shape, F32),
        name="hgrn_lower_bounds",
    )(p)


def _gdn_kernel(q_ref, k_ref, v_ref, z_ref, ab_ref, hist_ref, s0_ref, cw_ref, hp_ref, nw_ref,
                buf_ref, o_ref, sout_ref, prev_ref, s_ref, *, chunk):
    del buf_ref
    c = pl.program_id(1)
    C = chunk

    @pl.when(c == 0)
    def _():
        prev_ref[...] = jnp.zeros_like(prev_ref)
        prev_ref[:, :, C - SUBLANES:, :] = hist_ref[0]
        s_ref[...] = s0_ref[0]

    row = lax.broadcasted_iota(jnp.int32, (C, LANES), 0)
    lane = lax.broadcasted_iota(jnp.int32, (C, LANES), 1)
    ii = lax.broadcasted_iota(jnp.int32, (C, C), 0)
    jj = lax.broadcasted_iota(jnp.int32, (C, C), 1)
    eye = ii == jj
    tri = jj <= ii
    strict = jj < ii
    same16 = (ii >> 4) == (jj >> 4)
    eye_f = jnp.where(eye, 1.0, 0.0).astype(F32)

    ab = ab_ref[...]
    g_all = -jnp.exp(hp_ref[0:1, :]) * _softplus(ab + hp_ref[1:2, :])
    beta_all = _sigmoid(ab)
    nw = nw_ref[...]

    def conv(x, xp, w):
        y = x * w[CONV_W - 1:CONV_W, :]
        for j in range(1, CONV_W):
            xs = jnp.where(row < j, pltpu.roll(xp, j, 0), pltpu.roll(x, j, 0))
            y = y + xs * w[CONV_W - 1 - j:CONV_W - j, :]
        return _silu(y)

    def l2n(x):
        return x * lax.rsqrt(jnp.sum(x * x, axis=-1, keepdims=True) + RMS_EPS)

    def load(h):
        raw = [r[h] for r in (q_ref, k_ref, v_ref)]
        prev = [prev_ref[p, h] for p in range(3)]
        taps = [cw_ref[p, h] for p in range(3)]
        return raw, prev, taps, z_ref[h], s_ref[h]

    def store(h, raw, o, s_new):
        for p in range(3):
            prev_ref[p, h] = raw[p]
        s_ref[h] = s_new
        o_ref[h] = o

    def compute(h, raw, prev, taps, z, s):
        q = l2n(conv(raw[0], prev[0], taps[0])) * (HEAD_DIM ** -0.5)
        k = l2n(conv(raw[1], prev[1], taps[1]))
        v = conv(raw[2], prev[2], taps[2])
        g_col = jnp.sum(jnp.where(lane == h, g_all, 0.0), axis=1, keepdims=True)
        b_col = jnp.sum(jnp.where(lane == h + N_HEAD, beta_all, 0.0), axis=1, keepdims=True)

        g_row = jnp.sum(jnp.where(eye, g_col, 0.0), axis=0, keepdims=True)
        gc_col = jnp.sum(jnp.where(tri, g_row, 0.0), axis=1, keepdims=True)
        gc_row = jnp.sum(jnp.where(eye, gc_col, 0.0), axis=0, keepdims=True)
        diff = gc_col - gc_row
        decay = jnp.where(tri, jnp.exp(jnp.where(tri, diff, 0.0)), 0.0)
        egc = jnp.exp(gc_col)
        g_last = gc_col[C - 1:C, :]

        kb = k * b_col
        rhs = jnp.concatenate([v * b_col, kb * egc], axis=1)
        k_dec = k * jnp.exp(g_last - gc_col)
        yield
        qk_kbk = _dot_nt(jnp.concatenate([q, kb], axis=0), k)
        attn = jnp.where(tri, qk_kbk[:C] * decay, 0.0)
        lower = jnp.where(strict, qk_kbk[C:] * decay, 0.0)

        ld = jnp.where(same16, lower, 0.0)
        lo = lower - ld
        yield
        p2 = _dot(ld, ld)
        yield
        p4 = _dot(p2, p2)
        a1 = _dot(eye_f - ld, eye_f + p2)
        yield
        p8 = _dot(p4, p4)
        a2 = _dot(a1, eye_f + p4)
        yield
        td = _dot(a2, eye_f + p8)
        yield
        y_nm = _dot(td, jnp.concatenate([rhs, lo], axis=1))
        y = y_nm[:, :2 * HEAD_DIM]
        nm = y_nm[:, 2 * HEAD_DIM:]
        yield
        n2 = _dot(nm, nm)
        yield
        zz = y + _dot(n2, y)
        yield
        sol = zz - _dot(nm, zz)
        u = sol[:, :HEAD_DIM]
        w = sol[:, HEAD_DIM:]
        yield
        ws_qs = _dot(jnp.concatenate([w, q * egc], axis=0), s)
        v_new = u - ws_qs[:C]
        yield
        o = ws_qs[C:] + _dot(attn, v_new)
        s_new = s * jnp.exp(g_last) + _dot_tn(k_dec, v_new)

        o = o * lax.rsqrt(jnp.mean(o * o, axis=-1, keepdims=True) + RMS_EPS) * nw
        return (o * _silu(z)).astype(o_ref.dtype), s_new

    def group(i, carry):
        heads = [i * GDN_HEADS_PER_STEP + u for u in range(GDN_HEADS_PER_STEP)]
        loaded = [load(h) for h in heads]
        results = _run_interleaved([compute(h, *vals) for h, vals in zip(heads, loaded)])
        for h, vals, (o, s_new) in zip(heads, loaded, results):
            store(h, vals[0], o, s_new)
        return carry

    lax.fori_loop(0, N_HEAD // GDN_HEADS_PER_STEP, group, 0)

    @pl.when(c == pl.num_programs(1) - 1)
    def _():
        sout_ref[0] = s_ref[...]


def _gdn(proj, ab, hist, s0, cw, hp, nw, buf, layer, *, batch, row0, t_pad, chunk):
    nc = t_pad // chunk
    rb0 = row0 // chunk
    l_state = layer if hist.shape[0] > 1 else 0

    def sec(s):
        return pl.BlockSpec((N_HEAD, chunk, LANES), lambda b, c: (s, rb0 + b * nc + c, 0))

    kern = functools.partial(_gdn_kernel, chunk=chunk)
    return pl.pallas_call(
        kern,
        grid=(batch, nc),
        in_specs=[sec(SEC_GQ), sec(SEC_GK), sec(SEC_GV), sec(SEC_GZ),
                  pl.BlockSpec((chunk, LANES), lambda b, c: (rb0 + b * nc + c, 0)),
                  pl.BlockSpec((None, 1, 3, N_HEAD, SUBLANES, LANES),
                               lambda b, c: (l_state, b, 0, 0, 0, 0)),
                  pl.BlockSpec((None, 1, N_HEAD, HEAD_DIM, HEAD_DIM),
                               lambda b, c: (l_state, b, 0, 0, 0)),
                  pl.BlockSpec((None, 3, N_HEAD, SUBLANES, LANES), lambda b, c: (layer, 0, 0, 0, 0)),
                  pl.BlockSpec((None, SUBLANES, LANES), lambda b, c: (layer, 0, 0)),
                  pl.BlockSpec((None, 1, LANES), lambda b, c: (layer, 0, 0)),
                  pl.BlockSpec(memory_space=pl.ANY)],
        out_specs=[pl.BlockSpec((None, N_HEAD, chunk, LANES),
                                lambda b, c: (BR_GDN, 0, rb0 + b * nc + c, 0)),
                   pl.BlockSpec((1, N_HEAD, HEAD_DIM, HEAD_DIM), lambda b, c: (b, 0, 0, 0))],
        out_shape=[jax.ShapeDtypeStruct(buf.shape, buf.dtype),
                   jax.ShapeDtypeStruct((batch, N_HEAD, HEAD_DIM, HEAD_DIM), F32)],
        input_output_aliases={10: 0},
        scratch_shapes=[pltpu.VMEM((3, N_HEAD, chunk, LANES), F32),
                        pltpu.VMEM((N_HEAD, HEAD_DIM, HEAD_DIM), F32)],
        compiler_params=pltpu.CompilerParams(
            dimension_semantics=("arbitrary", "arbitrary"),
            vmem_limit_bytes=32 * 1024 * 1024),
        name="gdn_mixer",
    )(proj, proj, proj, proj, ab, hist, s0, cw, hp, nw, buf)


def _hgrn_kernel(q_ref, f_ref, i_ref, g_ref, lb_ref, s0_ref, nw_ref, buf_ref, o_ref, sout_ref,
                 s_ref, *, chunk):
    del buf_ref
    c = pl.program_id(1)
    C = chunk
    halves = [r for r in (32, 16, 8) if 2 * r <= C]

    @pl.when(c == 0)
    def _():
        s_ref[...] = s0_ref[0]

    ii = lax.broadcasted_iota(jnp.int32, (C, C), 0)
    jj = lax.broadcasted_iota(jnp.int32, (C, C), 1)
    sel_blocks = [jnp.where(jj <= ii, 1.0, 0.0)]
    level_masks = []
    for r in halves:
        ref_row = (ii & ~(2 * r - 1)) + (r - 1)
        sel_blocks.append(jnp.where(jj <= ref_row, 1.0, 0.0))
        same_pair = (ii & ~(2 * r - 1)) == (jj & ~(2 * r - 1))
        level_masks.append(same_pair & ((ii & r) != 0) & ((jj & r) == 0))
    sel = jnp.concatenate(sel_blocks, axis=0).astype(BF16)
    nw = nw_ref[...]
    row8 = lax.broadcasted_iota(jnp.int32, (SUBLANES, LANES), 0)

    def load(h):
        return q_ref[h], f_ref[h], i_ref[h], g_ref[h], lb_ref[h], s_ref[h]

    def compute(q_raw, f_raw, v, g_raw, lb, st):
        q = _silu(q_raw)
        f = lb + (1.0 - lb) * _sigmoid(f_raw)
        logf = jnp.log(f)
        k = 1.0 - f
        yield
        cums = _dot_sel_lhs(sel, logf)
        b = cums[:C]
        q_dec = q * jnp.exp(b)
        b_last = b[C - 1:C]
        k_dec = k * jnp.exp(b_last - b)
        yield
        attn = jnp.zeros((C, C), F32)
        for l, mask in enumerate(level_masks):
            bref = cums[(l + 1) * C:(l + 2) * C]
            qt = q * jnp.exp(jnp.minimum(b - bref, 0.0))
            kt = k * jnp.exp(jnp.minimum(bref - b, 0.0))
            attn = attn + jnp.where(mask, _dot_nt(qt, kt), 0.0)
        o_inter = _dot_nt(q_dec, st)
        st_new = st * jnp.exp(b_last) + _dot_tn(v, k_dec)
        yield
        o = o_inter + _dot(attn, v)

        diag = []
        for d in range(C // SUBLANES):
            sl = slice(d * SUBLANES, (d + 1) * SUBLANES)
            qd, kd, bd, vd = q[sl], k[sl], b[sl], v[sl]
            od = jnp.zeros((SUBLANES, LANES), F32)
            for s in range(SUBLANES):
                e = jnp.exp(jnp.minimum(bd - bd[s:s + 1], 0.0))
                wgt = jnp.sum(qd * kd[s:s + 1] * e, axis=1, keepdims=True)
                od = od + jnp.where(row8 >= s, wgt, 0.0) * vd[s:s + 1]
            diag.append(od)
        o = o + jnp.concatenate(diag, axis=0)

        o = o * lax.rsqrt(jnp.mean(o * o, axis=-1, keepdims=True) + RMS_EPS) * nw
        return (o * _silu(g_raw)).astype(o_ref.dtype), st_new

    def group(i, carry):
        heads = [i * HGRN_HEADS_PER_STEP + u for u in range(HGRN_HEADS_PER_STEP)]
        results = _run_interleaved([compute(*vals) for vals in [load(h) for h in heads]])
        for h, (o, st_new) in zip(heads, results):
            s_ref[h] = st_new
            o_ref[h] = o
        return carry

    lax.fori_loop(0, N_HEAD // HGRN_HEADS_PER_STEP, group, 0)

    @pl.when(c == pl.num_programs(1) - 1)
    def _():
        sout_ref[0] = s_ref[...]


def _hgrn(proj, lb, s0_t, nw, buf, layer, *, batch, row0, t_pad, chunk):
    nc = t_pad // chunk
    rb0 = row0 // chunk
    l_state = layer if s0_t.shape[0] > 1 else 0

    def sec(s):
        return pl.BlockSpec((N_HEAD, chunk, LANES), lambda b, c: (s, rb0 + b * nc + c, 0))

    kern = functools.partial(_hgrn_kernel, chunk=chunk)
    return pl.pallas_call(
        kern,
        grid=(batch, nc),
        in_specs=[sec(SEC_HQ), sec(SEC_HF), sec(SEC_HI), sec(SEC_HG),
                  pl.BlockSpec((None, N_HEAD, 1, LANES), lambda b, c: (layer, 0, 0, 0)),
                  pl.BlockSpec((None, 1, N_HEAD, HEAD_DIM, HEAD_DIM),
                               lambda b, c: (l_state, b, 0, 0, 0)),
                  pl.BlockSpec((None, 1, LANES), lambda b, c: (layer, 0, 0)),
                  pl.BlockSpec(memory_space=pl.ANY)],
        out_specs=[pl.BlockSpec((None, N_HEAD, chunk, LANES),
                                lambda b, c: (BR_HGRN, 0, rb0 + b * nc + c, 0)),
                   pl.BlockSpec((1, N_HEAD, HEAD_DIM, HEAD_DIM), lambda b, c: (b, 0, 0, 0))],
        out_shape=[jax.ShapeDtypeStruct(buf.shape, buf.dtype),
                   jax.ShapeDtypeStruct((batch, N_HEAD, HEAD_DIM, HEAD_DIM), F32)],
        input_output_aliases={7: 0},
        scratch_shapes=[pltpu.VMEM((N_HEAD, HEAD_DIM, HEAD_DIM), F32)],
        compiler_params=pltpu.CompilerParams(
            dimension_semantics=("arbitrary", "arbitrary"),
            vmem_limit_bytes=32 * 1024 * 1024),
        name="hgrn_mixer",
    )(proj, proj, proj, proj, lb, s0_t, nw, buf)


def _sb_kernel(*refs, q_block, t_new, t_past, pad_front):
    if t_past:
        q_ref, kn_ref, vn_ref, kp_ref, vp_ref, _, o_ref, ko_ref, vo_ref, kb_ref, vb_ref = refs
    else:
        q_ref, kn_ref, vn_ref, _, o_ref, ko_ref, vo_ref, kb_ref, vb_ref = refs
    qi = pl.program_id(2)
    t_all = t_past + t_new
    t_scr = kb_ref.shape[0]
    n_sub = SB_KEY_SLAB // LANES

    @pl.when(qi == 0)
    def _():
        k_new = kn_ref[0]
        v_new = vn_ref[0]
        ko_ref[0] = k_new[pad_front:, :]
        vo_ref[0] = v_new[pad_front:, :]
        if t_past:
            head_rows = pl.ds(pl.program_id(1), t_past, stride=N_HEAD)
            kb_ref[0:t_past, :] = kp_ref[0, head_rows, :].astype(BF16)
            vb_ref[0:t_past, :] = vp_ref[0, head_rows, :].astype(BF16)
        kb_ref[t_past:t_all, :] = k_new.astype(BF16)
        vb_ref[t_past:t_all, :] = v_new.astype(BF16)
        if t_scr > t_all:
            kb_ref[t_all:, :] = jnp.zeros((t_scr - t_all, LANES), BF16)
            vb_ref[t_all:, :] = jnp.zeros((t_scr - t_all, LANES), BF16)

    n_part = 2 if q_block % (4 * SUBLANES * 2) == 0 and q_block >= LANES else 1
    rows = q_block // n_part

    q = q_ref[0].astype(BF16)
    q_pos = t_past + qi * q_block + lax.broadcasted_iota(jnp.int32, (rows, SB_KEY_SLAB), 0)
    lane = lax.broadcasted_iota(jnp.int32, (rows, SB_KEY_SLAB), 1)
    jr = lax.broadcasted_iota(jnp.int32, (LANES, 2 * LANES), 0)
    sc = lax.broadcasted_iota(jnp.int32, (LANES, 2 * LANES), 1)
    suffix_and_total = jnp.where((sc >= LANES) | (jr > sc), 1.0, 0.0).astype(BF16)
    slab_hi = (t_past + qi * q_block + q_block - 1) // SB_KEY_SLAB
    scale = HEAD_DIM ** -0.5

    def cond(carry):
        it, _, _, active = carry
        return jnp.logical_and(it <= slab_hi, active > 0)

    def part(p, slab, kt, vt, run, acc):
        sl = slice(p * rows, (p + 1) * rows)
        z = lax.dot_general(q[sl], kt, (((1,), (1,)), ((), ())), preferred_element_type=F32) * scale
        yield
        k_pos = slab * SB_KEY_SLAB + lane
        vis = (k_pos < q_pos + p * rows) & (k_pos >= pad_front)
        sp = _softplus(z)
        log_keep = jnp.where(vis, -sp, 0.0)
        between = [None] * n_sub
        for j in reversed(range(n_sub)):
            sums = _dot_sel_rhs(log_keep[:, j * LANES:(j + 1) * LANES], suffix_and_total)
            between[j] = sums[:, :LANES] + run
            run = run + sums[:, LANES:]
        yield
        a = jnp.where(vis, jnp.exp((z - sp) + jnp.concatenate(between, axis=1)), 0.0)
        acc = acc + jnp.dot(a.astype(BF16), vt, preferred_element_type=F32)
        return run, acc

    def body(carry):
        it, run, acc, _ = carry
        slab = slab_hi - it
        start = pl.multiple_of(slab * SB_KEY_SLAB, SB_KEY_SLAB)
        kt = kb_ref[pl.ds(start, SB_KEY_SLAB), :]
        vt = vb_ref[pl.ds(start, SB_KEY_SLAB), :]
        done = _run_interleaved([
            part(p, slab, kt, vt, run[p * rows:(p + 1) * rows], acc[p * rows:(p + 1) * rows])
            for p in range(n_part)])
        run = jnp.concatenate([r for r, _ in done], axis=0)
        acc = jnp.concatenate([a for _, a in done], axis=0)
        return it + 1, run, acc, (jnp.max(run) > EXP_UNDERFLOW).astype(jnp.int32)

    zero = jnp.zeros((q_block, LANES), F32)
    _, _, acc, _ = lax.while_loop(cond, body, (jnp.int32(0), zero, zero, jnp.int32(1)))
    o_ref[...] = acc.astype(o_ref.dtype)


def _sb(proj, past_k, past_v, buf, layer, *, batch, row0, t_pad, q_block, t_past, pad_front):
    nq = t_pad // q_block
    rbq0 = row0 // q_block
    rbk0 = row0 // t_pad
    t_scr = -(-(t_past + t_pad) // SB_KEY_SLAB) * SB_KEY_SLAB
    t_out = t_pad - pad_front
    sec_q = SEC_SQ * N_HEAD
    sec_k = SEC_SK * N_HEAD
    sec_v = SEC_SV * N_HEAD
    in_specs = [pl.BlockSpec((1, q_block, LANES), lambda b, h, i: (sec_q + h, rbq0 + b * nq + i, 0)),
                pl.BlockSpec((1, t_pad, LANES), lambda b, h, i: (sec_k + h, rbk0 + b, 0)),
                pl.BlockSpec((1, t_pad, LANES), lambda b, h, i: (sec_v + h, rbk0 + b, 0))]
    args = [proj, proj, proj]
    if t_past:
        past_spec = pl.BlockSpec((None, 1, t_past * N_HEAD, LANES), lambda b, h, i: (layer, b, 0, 0))
        in_specs += [past_spec, past_spec]
        args += [past_k, past_v]
    in_specs.append(pl.BlockSpec(memory_space=pl.ANY))
    args.append(buf)
    kv_out = pl.BlockSpec((1, t_out, LANES), lambda b, h, i: (b, 0, h))
    kern = functools.partial(_sb_kernel, q_block=q_block, t_new=t_pad, t_past=t_past,
                             pad_front=pad_front)
    return pl.pallas_call(
        kern,
        grid=(batch, N_HEAD, nq),
        in_specs=in_specs,
        out_specs=[pl.BlockSpec((None, None, q_block, LANES),
                                lambda b, h, i: (BR_SB, h, rbq0 + b * nq + i, 0)),
                   kv_out, kv_out],
        input_output_aliases={len(args) - 1: 0},
        out_shape=[jax.ShapeDtypeStruct(buf.shape, buf.dtype),
                   jax.ShapeDtypeStruct((batch, t_out, MIX_WIDTH), F32),
                   jax.ShapeDtypeStruct((batch, t_out, MIX_WIDTH), F32)],
        scratch_shapes=[pltpu.VMEM((t_scr, LANES), BF16), pltpu.VMEM((t_scr, LANES), BF16)],
        compiler_params=pltpu.CompilerParams(
            dimension_semantics=("arbitrary", "arbitrary", "arbitrary"),
            vmem_limit_bytes=_vmem_limit(2 * t_past * N_HEAD * LANES * 4, 4 * t_pad * LANES * 4)),
        name="stickbreak_attention",
    )(*args)


def _last_rows(proj_a, row_lo, row_hi, t_seq):
    n_qkv = 3 * N_HEAD
    rows = [lax.slice(proj_a, (0, row_lo + t_seq - (CONV_W - 1) + r, 0),
                      (n_qkv, row_hi, HEAD_DIM), (1, t_seq, 1)) for r in range(CONV_W - 1)]
    taps = jnp.stack(rows)
    return taps.transpose(2, 0, 1, 3).reshape(taps.shape[2], CONV_W - 1, 3 * MIX_WIDTH)


def kernel(x_prompt, x_sample, cache_sb_k, cache_sb_v, state_gdn_conv, state_gdn, state_hgrn,
           meta_tokens, norm1_w, w_in, gdn_conv_w, gdn_a_log, gdn_dt_bias, gdn_norm_w,
           hgrn_lb_param, hgrn_norm_w, w_branch, w_o, norm2_w, w_ff_gate, w_ff_up, w_ff_down,
           final_norm_w):
    b_p, seq, _ = x_prompt.shape
    b_s, t_s, _ = x_sample.shape
    t_p = N_META + seq
    sb_qb = SB_Q_BLOCK_P if t_p > SB_Q_BLOCK_P else CHUNK_P
    t_pp = -(-t_p // sb_qb) * sb_qb
    pad_front = t_pp - t_p
    m_p = b_p * t_pp
    m_s = b_s * t_s
    m_all = m_p + m_s
    past_len = cache_sb_k.shape[2]
    tm = _row_tile(m_all, 1088)
    t_skip = pad_front + N_META
    assert t_pp % CHUNK_P == 0 and pad_front % SUBLANES == 0 and t_skip % CHUNK_P == 0

    meta = jnp.broadcast_to(meta_tokens[None], (b_p, N_META, D_MODEL)).astype(F32)
    xp = jnp.concatenate([jnp.zeros((b_p, pad_front, D_MODEL), F32), meta, x_prompt], axis=1)
    x = jnp.concatenate([xp.reshape(m_p, D_MODEL), x_sample.reshape(m_s, D_MODEL)], axis=0)

    w_in_a = w_in[:, :, :_W_IN_AB].astype(BF16)
    w_in_b = w_in[:, :, _W_IN_SB:_W_IN_GATE].astype(BF16)
    w_in_g = w_in[:, :, _W_IN_GATE:].astype(BF16)
    w_in_ab = jnp.pad(w_in[:, :, _W_IN_AB:_W_IN_SB],
                      ((0, 0), (0, 0), (0, LANES - 2 * N_HEAD))).astype(BF16)
    w_br = _cast_bf16(w_branch.reshape(DEPTH * N_BRANCH, MIX_WIDTH, D_MODEL), block=(256, D_MODEL))
    w_out = _cast_bf16(w_o, block=(256, D_MODEL))
    w_fg = _cast_bf16(w_ff_gate, block=(D_MODEL, 256), out_cols=D_FF_PAD)
    w_fu = _cast_bf16(w_ff_up, block=(D_MODEL, 256), out_cols=D_FF_PAD)
    w_fd = _cast_bf16(w_ff_down, block=(256, D_MODEL), out_rows=D_FF_PAD)

    lower_bounds = _lower_bounds(hgrn_lb_param).reshape(DEPTH, N_HEAD, 1, HEAD_DIM)
    cw = gdn_conv_w.reshape(DEPTH, CONV_W, 3, N_HEAD, HEAD_DIM).transpose(0, 2, 3, 1, 4)
    cw = jnp.pad(cw, ((0, 0), (0, 0), (0, 0), (0, SUBLANES - CONV_W), (0, 0)))
    hp = jnp.zeros((DEPTH, SUBLANES, LANES), F32)
    hp = hp.at[:, 0, :N_HEAD].set(gdn_a_log).at[:, 1, :N_HEAD].set(gdn_dt_bias)
    gnw = gdn_norm_w.reshape(DEPTH, 1, HEAD_DIM)
    hnw = hgrn_norm_w.reshape(DEPTH, 1, HEAD_DIM)
    hist_s = state_gdn_conv.reshape(DEPTH, b_s, CONV_W - 1, 3, N_HEAD, HEAD_DIM)
    hist_s = jnp.pad(hist_s.transpose(0, 1, 3, 4, 2, 5),
                     ((0, 0),) * 4 + ((SUBLANES - (CONV_W - 1), 0), (0, 0)))
    zero_hist = jnp.zeros((1, b_p, 3, N_HEAD, SUBLANES, LANES), F32)
    zero_state = jnp.zeros((1, b_p, N_HEAD, HEAD_DIM, HEAD_DIM), F32)
    hgrn_s0 = state_hgrn.swapaxes(-1, -2)
    past_k = cache_sb_k.reshape(DEPTH, b_s, past_len * N_HEAD, HEAD_DIM)
    past_v = cache_sb_v.reshape(DEPTH, b_s, past_len * N_HEAD, HEAD_DIM)
    branches = jnp.zeros((N_BRANCH, N_HEAD, m_all, LANES), BF16)

    outs = {name: [] for name in ("kp", "vp", "ks", "vs", "cp", "cs", "gp", "gs", "hp", "hs")}
    for l in range(DEPTH):
        h = _rmsnorm(x, norm1_w[l], BF16)
        proj_a = _dense(h, w_in_a, l, tm=tm, tn=1024, mode="headmajor", name="proj_gdn")
        proj_b = _dense(h, w_in_b, l, tm=tm, tn=1024, mode="headmajor", name="proj_sb_hgrn")
        ab = _dense(h, w_in_ab, l, tm=tm, tn=LANES, name="proj_ab")
        gates = _dense(h, w_in_g, l, tm=tm, tn=1024, mode="sigmoid", out_dtype=BF16, name="proj_gates")

        branches, gs_p = _gdn(proj_a, ab, zero_hist, zero_state, cw, hp, gnw, branches, l,
                              batch=b_p, row0=0, t_pad=t_pp, chunk=CHUNK_P)
        branches, gs_s = _gdn(proj_a, ab, hist_s, state_gdn, cw, hp, gnw, branches, l,
                              batch=b_s, row0=m_p, t_pad=t_s, chunk=t_s)
        branches, k_p, v_p = _sb(proj_b, None, None, branches, l, batch=b_p, row0=0, t_pad=t_pp,
                                 q_block=sb_qb, t_past=0, pad_front=pad_front)
        branches, k_s, v_s = _sb(proj_b, past_k, past_v, branches, l, batch=b_s, row0=m_p,
                                 t_pad=t_s, q_block=t_s, t_past=past_len, pad_front=0)
        branches, hs_p = _hgrn(proj_b, lower_bounds, zero_state, hnw, branches, l,
                               batch=b_p, row0=0, t_pad=t_pp, chunk=CHUNK_P)
        branches, hs_s = _hgrn(proj_b, lower_bounds, hgrn_s0, hnw, branches, l,
                               batch=b_s, row0=m_p, t_pad=t_s, chunk=t_s)

        merged = _merge(branches, w_br, l, gates, tm=tm, tn=1024)
        x = _dense(merged, w_out, l, tm=tm, tn=1024, mode="residual", residual=x, name="out_proj")

        h2 = _rmsnorm(x, norm2_w[l], BF16)
        hmid = _dense(h2, w_fg, l, tm=tm, tn=512, mode="swiglu", out_dtype=BF16, w2=w_fu,
                      name="ffn_gate_up")
        x = _dense_ktiled_residual(hmid, w_fd, l, x, tm=tm, tn=1024, tk=1408, name="ffn_down")

        outs["kp"].append(k_p.reshape(b_p, t_p, N_HEAD, HEAD_DIM))
        outs["vp"].append(v_p.reshape(b_p, t_p, N_HEAD, HEAD_DIM))
        outs["ks"].append(k_s.reshape(b_s, t_s, N_HEAD, HEAD_DIM))
        outs["vs"].append(v_s.reshape(b_s, t_s, N_HEAD, HEAD_DIM))
        outs["cp"].append(_last_rows(proj_a, 0, m_p, t_pp))
        outs["cs"].append(_last_rows(proj_a, m_p, m_all, t_s))
        outs["gp"].append(gs_p)
        outs["gs"].append(gs_s)
        outs["hp"].append(hs_p.swapaxes(-1, -2))
        outs["hs"].append(hs_s.swapaxes(-1, -2))

    tf = CHUNK_P
    y_prompt = _rmsnorm(
        x, final_norm_w, F32, tm=tf, grid=(b_p, seq // tf),
        in_map=lambda b, i: (b * (t_pp // tf) + t_skip // tf + i, 0),
        out_map=lambda b, i: (b * (seq // tf) + i, 0), out_rows=b_p * seq).reshape(b_p, seq, D_MODEL)
    ts = _row_tile(m_s, 256)
    y_sample = _rmsnorm(
        x, final_norm_w, F32, tm=ts, grid=(m_s // ts,),
        in_map=lambda i: (m_p // ts + i, 0), out_map=lambda i: (i, 0),
        out_rows=m_s).reshape(b_s, t_s, D_MODEL)
    st = {k: jnp.stack(v) for k, v in outs.items()}
    return (y_prompt, y_sample, st["kp"], st["vp"], st["ks"], st["vs"], st["cp"], st["cs"],
            st["gp"], st["gs"], st["hp"], st["hs"])
```
